```python
import jax
import jax.numpy as jnp
from jax import lax
import numpy as np

D_MODEL = 2048
BATCH = 1
SEQ = 8192
DEPTH = 4
DEC_BATCH = 4
DEC_SEQ = 4096
PAST_LEN = 128

N_MIXERS = 3
HEAD_DIM = 128
NORM_EPS = 1e-6
A_HEADS = 16
A_KV_HEADS = 4
A_WINDOW = 128
B_HEADS = 16
B_GROUPS = ((128, 1), (512, 4), (2048, 16))
C_HEADS = 16
C_Q_RANK = 512
C_KV_RANK = 512
C_NOPE = 128
C_ROPE = 64
C_V = 128
C_QBLOCK = 128
ROPE_THETA = 10000.0
D_FF = 5632
CONV_WIDTH = 3

kernel_name = 'hybrid_bidir_encoder_swa_dilated_mla_convglu'


def _rmsnorm(x, g):
    xf = x.astype(jnp.float32)
    y = xf * lax.rsqrt(jnp.mean(xf * xf, axis=-1, keepdims=True) + NORM_EPS)
    return (y * g.astype(jnp.float32)).astype(x.dtype)


def _alibi_slopes(n):
    return jnp.power(2.0, -8.0 * jnp.arange(1, n + 1, dtype=jnp.float32) / n)


def _band_blocks(t, blk):
    n, lp = t.shape[0], t.shape[1]
    nb = lp // blk
    pad = [(0, 0), (blk, blk)] + [(0, 0)] * (t.ndim - 2)
    tp = jnp.pad(t, pad).reshape((n, nb + 2, blk) + t.shape[2:])
    return jnp.concatenate([tp[:, :-2], tp[:, 1:-1], tp[:, 2:]], axis=2)


def _banded_attention(q, k, v, half, slopes, step, sink=None):
    n, length, kvh, g, dh = q.shape
    blk = half
    lp = -(-length // blk) * blk
    if lp != length:
        pw = lp - length
        q = jnp.pad(q, [(0, 0), (0, pw), (0, 0), (0, 0), (0, 0)])
        k = jnp.pad(k, [(0, 0), (0, pw), (0, 0), (0, 0)])
        v = jnp.pad(v, [(0, 0), (0, pw), (0, 0), (0, 0)])
    nb = lp // blk
    qb = q.reshape(n, nb, blk, kvh, g, dh)
    kb = _band_blocks(k, blk)
    vb = _band_blocks(v, blk)
    s = jnp.einsum('nbqkgd,nbjkd->nbkgqj', qb, kb).astype(jnp.float32) * (dh ** -0.5)
    qi = jnp.arange(blk)[:, None]
    kj = jnp.arange(3 * blk)[None, :]
    rel = qi + blk - kj
    kpos = jnp.arange(nb)[:, None, None] * blk + kj[None] - blk
    valid = (jnp.abs(rel) <= half)[None] & (kpos >= 0) & (kpos < length)
    bias = -(slopes.astype(jnp.float32) * step)[:, :, None, None] * jnp.abs(rel).astype(jnp.float32)
    s = jnp.where(valid[None, :, None, None], s + bias, -jnp.inf)
    if sink is not None:
        sink_col = jnp.broadcast_to(sink.astype(jnp.float32)[:, :, None, None], s.shape[:-1] + (1,))
        s = jnp.concatenate([s, sink_col], axis=-1)
    lse = jax.nn.logsumexp(s, axis=-1)
    p = jnp.exp(s - lse[..., None])
    if sink is not None:
        p = p[..., :-1]
    out = jnp.einsum('nbkgqj,nbjkd->nbqkgd', p.astype(v.dtype), vb).reshape(n, lp, kvh, g, dh)[:, :length]
    lse = jnp.transpose(lse, (0, 1, 4, 2, 3)).reshape(n, lp, kvh, g)[:, :length]
    return out, lse


def _mixer_a(h, w_qkv, sink, w_o):
    b, s, _ = h.shape
    g = A_HEADS // A_KV_HEADS
    nq, nk = A_HEADS * HEAD_DIM, A_KV_HEADS * HEAD_DIM
    qkv = h @ w_qkv
    q = qkv[..., :nq].reshape(b, s, A_KV_HEADS, g, HEAD_DIM)
    k = qkv[..., nq:nq + nk].reshape(b, s, A_KV_HEADS, HEAD_DIM)
    v = qkv[..., nq + nk:].reshape(b, s, A_KV_HEADS, HEAD_DIM)
    slopes = _alibi_slopes(A_HEADS).reshape(A_KV_HEADS, g)
    out, _ = _banded_attention(q, k, v, A_WINDOW, slopes, 1, sink.reshape(A_KV_HEADS, g))
    return out.reshape(b, s, nq) @ w_o


def _to_strided(t, dil):
    b, s = t.shape[0], t.shape[1]
    t = t.reshape((b, s // dil, dil) + t.shape[2:])
    return jnp.moveaxis(t, 2, 1).reshape((b * dil, s // dil) + t.shape[3:])


def _from_strided(t, b, dil):
    l = t.shape[1]
    t = t.reshape((b, dil, l) + t.shape[2:])
    return jnp.moveaxis(t, 1, 2).reshape((b, l * dil) + t.shape[3:])


def _mixer_b(h, w_qkv, w_o):
    b, s, _ = h.shape
    qkv = (h @ w_qkv).reshape(b, s, len(B_GROUPS), 3, B_HEADS, HEAD_DIM)
    slopes = _alibi_slopes(B_HEADS).reshape(B_HEADS, 1)
    outs, lses = [], []
    for gi, (window, dil) in enumerate(B_GROUPS):
        half = (window // 2) // dil
        q = _to_strided(qkv[:, :, gi, 0], dil)
        k = _to_strided(qkv[:, :, gi, 1], dil)
        v = _to_strided(qkv[:, :, gi, 2], dil)
        o, lse = _banded_attention(q[:, :, :, None], k, v, half, slopes, dil)
        outs.append(_from_strided(o[:, :, :, 0], b, dil))
        lses.append(_from_strided(lse[..., 0], b, dil))
    alpha = jax.nn.softmax(jnp.stack(lses), axis=0)
    out = jnp.einsum('gbsh,gbshd->bshd', alpha, jnp.stack(outs).astype(jnp.float32))
    return out.astype(h.dtype).reshape(b, s, B_HEADS * HEAD_DIM) @ w_o


def _rope_tables(s):
    pos = jnp.arange(s, dtype=jnp.float32)
    inv = jnp.power(ROPE_THETA, -jnp.arange(0, C_ROPE, 2, dtype=jnp.float32) / C_ROPE)
    ang = pos[:, None] * inv[None, :]
    return jnp.cos(ang), jnp.sin(ang)


def _apply_rope(x, cos, sin):
    xf = x.astype(jnp.float32)
    x1, x2 = xf[..., :C_ROPE // 2], xf[..., C_ROPE // 2:]
    c, sn = cos[None, :, None, :], sin[None, :, None, :]
    return jnp.concatenate([x1 * c - x2 * sn, x2 * c + x1 * sn], axis=-1).astype(x.dtype)


def _mixer_c(h, w_down, q_norm, kv_norm, w_uq, w_ukv, w_o):
    b, s, _ = h.shape
    c = h @ w_down
    cq = _rmsnorm(c[..., :C_Q_RANK], q_norm)
    ckv = _rmsnorm(c[..., C_Q_RANK:C_Q_RANK + C_KV_RANK], kv_norm)
    cos, sin = _rope_tables(s)
    k_rope = _apply_rope(c[..., C_Q_RANK + C_KV_RANK:][:, :, None, :], cos, sin)[:, :, 0]
    q = (cq @ w_uq).reshape(b, s, C_HEADS, C_NOPE + C_ROPE)
    kv = (ckv @ w_ukv).reshape(b, s, C_HEADS, C_NOPE + C_V)
    q_nope = q[..., :C_NOPE]
    q_rope = _apply_rope(q[..., C_NOPE:], cos, sin)
    k_nope, v = kv[..., :C_NOPE], kv[..., C_NOPE:]
    scale = (C_NOPE + C_ROPE) ** -0.5
    nb = s // C_QBLOCK

    def to_blocks(t):
        return jnp.moveaxis(t.reshape((b, nb, C_QBLOCK) + t.shape[2:]), 1, 0)

    def attend(blk):
        qn, qr = blk
        sc = (jnp.einsum('bqhd,bkhd->bhqk', qn, k_nope).astype(jnp.float32)
              + jnp.einsum('bqhr,bkr->bhqk', qr, k_rope).astype(jnp.float32)) * scale
        p = jax.nn.softmax(sc, axis=-1).astype(v.dtype)
        return jnp.einsum('bhqk,bkhd->bqhd', p, v)

    o = lax.map(attend, (to_blocks(q_nope), to_blocks(q_rope)))
    o = jnp.moveaxis(o, 0, 1).reshape(b, s, C_HEADS * C_V)
    return o @ w_o


def _conv_glu(x, norm, w_in, conv_w, conv_b, w_out):
    h = _rmsnorm(x, norm)
    u = h @ w_in
    a, val = u[..., :D_FF], u[..., D_FF:]
    a = lax.conv_general_dilated(
        a, conv_w[:, None, :].astype(a.dtype), window_strides=(1,),
        padding=((CONV_WIDTH // 2, CONV_WIDTH // 2),),
        dimension_numbers=('NWC', 'WIO', 'NWC'), feature_group_count=D_FF) + conv_b.astype(a.dtype)
    return (jax.nn.gelu(a, approximate=False) * val) @ w_out


def _trunk(x, layers, final_norm):
    mixers = (_mixer_a, _mixer_b, _mixer_c)
    for i in range(DEPTH):
        mix_norm, mix_params, ffn_params = layers[i]
        x = x + mixers[i % N_MIXERS](_rmsnorm(x, mix_norm), *mix_params)
        x = x + _conv_glu(x, *ffn_params)
    return _rmsnorm(x, final_norm)


def _dense(key, fan_in, fan_out):
    return jax.random.normal(key, (fan_in, fan_out), jnp.float32) * fan_in ** -0.5


def _gain(key, n):
    return 1.0 + 0.01 * jax.random.normal(key, (n,), jnp.float32)


def setup_inputs(seed: int = 0) -> dict:
    key = jax.random.key(seed)
    ks = jax.random.split(key, 64)
    cnt = [0]

    def nk():
        cnt[0] += 1
        return ks[cnt[0] - 1]

    p = {}
    p['x_prompt'] = jax.random.normal(nk(), (BATCH, SEQ, D_MODEL), jnp.float32)
    p['x_sample'] = jax.random.normal(nk(), (DEC_BATCH, DEC_SEQ, D_MODEL), jnp.float32)
    for i in range(DEPTH):
        pre = 'l%d_' % i
        kind = i % N_MIXERS
        p[pre + 'mix_norm'] = _gain(nk(), D_MODEL)
        if kind == 0:
            p[pre + 'a_w_qkv'] = _dense(nk(), D_MODEL, (A_HEADS + 2 * A_KV_HEADS) * HEAD_DIM)
            p[pre + 'a_sink'] = 0.5 * jax.random.normal(nk(), (A_HEADS,), jnp.float32)
            p[pre + 'a_w_o'] = _dense(nk(), A_HEADS * HEAD_DIM, D_MODEL)
        elif kind == 1:
            p[pre + 'b_w_qkv'] = _dense(nk(), D_MODEL, len(B_GROUPS) * 3 * B_HEADS * HEAD_DIM)
            p[pre + 'b_w_o'] = _dense(nk(), B_HEADS * HEAD_DIM, D_MODEL)
        else:
            p[pre + 'c_w_down'] = _dense(nk(), D_MODEL, C_Q_RANK + C_KV_RANK + C_ROPE)
            p[pre + 'c_q_norm'] = _gain(nk(), C_Q_RANK)
            p[pre + 'c_kv_norm'] = _gain(nk(), C_KV_RANK)
            p[pre + 'c_w_uq'] = _dense(nk(), C_Q_RANK, C_HEADS * (C_NOPE + C_ROPE))
            p[pre + 'c_w_ukv'] = _dense(nk(), C_KV_RANK, C_HEADS * (C_NOPE + C_V))
            p[pre + 'c_w_o'] = _dense(nk(), C_HEADS * C_V, D_MODEL)
        p[pre + 'ffn_norm'] = _gain(nk(), D_MODEL)
        p[pre + 'ffn_w_in'] = _dense(nk(), D_MODEL, 2 * D_FF)
        p[pre + 'ffn_conv_w'] = jax.random.normal(nk(), (CONV_WIDTH, D_FF), jnp.float32) * CONV_WIDTH ** -0.5
        p[pre + 'ffn_conv_b'] = 0.01 * jax.random.normal(nk(), (D_FF,), jnp.float32)
        p[pre + 'ffn_w_out'] = _dense(nk(), D_FF, D_MODEL)
    p['final_norm'] = _gain(nk(), D_MODEL)
    return p


def reference(x_prompt, x_sample,
              l0_mix_norm, l0_a_w_qkv, l0_a_sink, l0_a_w_o,
              l0_ffn_norm, l0_ffn_w_in, l0_ffn_conv_w, l0_ffn_conv_b, l0_ffn_w_out,
              l1_mix_norm, l1_b_w_qkv, l1_b_w_o,
              l1_ffn_norm, l1_ffn_w_in, l1_ffn_conv_w, l1_ffn_conv_b, l1_ffn_w_out,
              l2_mix_norm, l2_c_w_down, l2_c_q_norm, l2_c_kv_norm, l2_c_w_uq, l2_c_w_ukv, l2_c_w_o,
              l2_ffn_norm, l2_ffn_w_in, l2_ffn_conv_w, l2_ffn_conv_b, l2_ffn_w_out,
              l3_mix_norm, l3_a_w_qkv, l3_a_sink, l3_a_w_o,
              l3_ffn_norm, l3_ffn_w_in, l3_ffn_conv_w, l3_ffn_conv_b, l3_ffn_w_out,
              final_norm):
    layers = [
        (l0_mix_norm, (l0_a_w_qkv, l0_a_sink, l0_a_w_o),
         (l0_ffn_norm, l0_ffn_w_in, l0_ffn_conv_w, l0_ffn_conv_b, l0_ffn_w_out)),
        (l1_mix_norm, (l1_b_w_qkv, l1_b_w_o),
         (l1_ffn_norm, l1_ffn_w_in, l1_ffn_conv_w, l1_ffn_conv_b, l1_ffn_w_out)),
        (l2_mix_norm, (l2_c_w_down, l2_c_q_norm, l2_c_kv_norm, l2_c_w_uq, l2_c_w_ukv, l2_c_w_o),
         (l2_ffn_norm, l2_ffn_w_in, l2_ffn_conv_w, l2_ffn_conv_b, l2_ffn_w_out)),
        (l3_mix_norm, (l3_a_w_qkv, l3_a_sink, l3_a_w_o),
         (l3_ffn_norm, l3_ffn_w_in, l3_ffn_conv_w, l3_ffn_conv_b, l3_ffn_w_out)),
    ]
    y_prompt = _trunk(x_prompt, layers, final_norm)
    y_sample = _trunk(x_sample, layers, final_norm)
    return (y_prompt, y_sample)
```

```python
import functools

import jax
import jax.numpy as jnp
import numpy as np
from jax import lax
from jax.experimental import pallas as pl
from jax.experimental.pallas import tpu as pltpu

F32 = jnp.float32
BF16 = jnp.bfloat16

D_MODEL = 2048
HEAD_DIM = 128
NORM_EPS = 1e-6
A_HEADS = 16
A_KV_HEADS = 4
A_WINDOW = 128
B_HEADS = 16
B_GROUPS = ((128, 1), (512, 4), (2048, 16))
C_HEADS = 16
C_Q_RANK = 512
C_KV_RANK = 512
C_NOPE = 128
C_ROPE = 64
C_V = 128
ROPE_THETA = 10000.0
D_FF = 5632

VMEM_LIMIT_BYTES = 56 * 1024 * 1024
MASK_BIAS = 1e30


def _params(*sem):
    return pltpu.CompilerParams(dimension_semantics=sem, vmem_limit_bytes=VMEM_LIMIT_BYTES)


def _run_bounds(r0, plen, run_p, run_s):
    in_p = r0 < plen
    start = jnp.where(in_p, (r0 // run_p) * run_p, plen + ((r0 - plen) // run_s) * run_s)
    end = start + jnp.where(in_p, run_p, run_s)
    return start, end


def _rms_rows(x, g):
    ms = jnp.mean(x * x, axis=-1, keepdims=True)
    return (x * lax.rsqrt(ms + NORM_EPS)) * g


def _rms_to_scratch(x_ref, g_ref, hs_ref, row_off, nrows, chunk=64):
    g = g_ref[...]

    def body(c, carry):
        r = pl.multiple_of(c * chunk, chunk)
        y = _rms_rows(x_ref[pl.ds(r, chunk), :], g)
        hs_ref[pl.ds(row_off + r, chunk), :] = y.astype(hs_ref.dtype)
        return carry

    lax.fori_loop(0, nrows // chunk, body, 0)


def _norm_matmul_kernel(x_ref, g_ref, w_ref, cs_ref, o_ref, hs_ref, *, tm):
    @pl.when(pl.program_id(1) == 0)
    def _():
        _rms_to_scratch(x_ref, g_ref, hs_ref, 0, tm)

    acc = jnp.dot(hs_ref[...], w_ref[...], preferred_element_type=F32)
    o_ref[...] = (acc * cs_ref[...]).astype(o_ref.dtype)


def _norm_matmul(x, g, w, colscale, *, tm=1024, tn=1024, name):
    m, k = x.shape
    n = w.shape[1]
    return pl.pallas_call(
        functools.partial(_norm_matmul_kernel, tm=tm),
        grid=(m // tm, n // tn),
        in_specs=[
            pl.BlockSpec((tm, k), lambda i, j: (i, 0)),
            pl.BlockSpec((1, k), lambda i, j: (0, 0)),
            pl.BlockSpec((k, tn), lambda i, j: (0, j)),
            pl.BlockSpec((1, tn), lambda i, j: (0, j)),
        ],
        out_specs=pl.BlockSpec((tm, tn), lambda i, j: (i, j)),
        out_shape=jax.ShapeDtypeStruct((m, n), BF16),
        scratch_shapes=[pltpu.VMEM((tm, k), BF16)],
        compiler_params=_params("parallel", "arbitrary"),
        name=name,
    )(x, g.reshape(1, k), w, colscale.reshape(1, n))


def _matmul_res_kernel(a_ref, w_ref, x_ref, o_ref):
    o_ref[...] = x_ref[...] + jnp.dot(a_ref[...], w_ref[...], preferred_element_type=F32)


def _matmul_res(a, w, x, *, tm=1024, tn=1024, name):
    m, k = a.shape
    n = w.shape[1]
    return pl.pallas_call(
        _matmul_res_kernel,
        grid=(m // tm, n // tn),
        in_specs=[
            pl.BlockSpec((tm, k), lambda i, j: (i, 0)),
            pl.BlockSpec((k, tn), lambda i, j: (0, j)),
            pl.BlockSpec((tm, tn), lambda i, j: (i, j)),
        ],
        out_specs=pl.BlockSpec((tm, tn), lambda i, j: (i, j)),
        out_shape=jax.ShapeDtypeStruct((m, n), F32),
        compiler_params=_params("parallel", "arbitrary"),
        name=name,
    )(a, w, x)


def _merge_res_kernel(o1_ref, o2_ref, o3_ref, l1_ref, l2_ref, l3_ref, w_ref, x_ref, o_ref, a_ref,
                      *, heads):
    @pl.when(pl.program_id(1) == 0)
    def _():
        l1, l2, l3 = l1_ref[...], l2_ref[...], l3_ref[...]
        mx = jnp.maximum(jnp.maximum(l1, l2), l3)
        e1, e2, e3 = jnp.exp(l1 - mx), jnp.exp(l2 - mx), jnp.exp(l3 - mx)
        inv = 1.0 / (e1 + e2 + e3)
        a1, a2, a3 = e1 * inv, e2 * inv, e3 * inv
        for h in range(heads):
            cs = slice(h * HEAD_DIM, (h + 1) * HEAD_DIM)
            hs = slice(h, h + 1)
            acc = (a1[:, hs] * o1_ref[:, cs].astype(F32)
                   + a2[:, hs] * o2_ref[:, cs].astype(F32)
                   + a3[:, hs] * o3_ref[:, cs].astype(F32))
            a_ref[:, cs] = acc.astype(a_ref.dtype)

    o_ref[...] = x_ref[...] + jnp.dot(a_ref[...], w_ref[...], preferred_element_type=F32)


def _merge_res(os_, lses, w, x, *, tm=512, tn=1024, name):
    m, k = os_[0].shape
    n = w.shape[1]
    heads = lses[0].shape[1]
    o_spec = pl.BlockSpec((tm, k), lambda i, j: (i, 0))
    l_spec = pl.BlockSpec((tm, heads), lambda i, j: (i, 0))
    return pl.pallas_call(
        functools.partial(_merge_res_kernel, heads=heads),
        grid=(m // tm, n // tn),
        in_specs=[o_spec, o_spec, o_spec, l_spec, l_spec, l_spec,
                  pl.BlockSpec((k, tn), lambda i, j: (0, j)),
                  pl.BlockSpec((tm, tn), lambda i, j: (i, j))],
        out_specs=pl.BlockSpec((tm, tn), lambda i, j: (i, j)),
        out_shape=jax.ShapeDtypeStruct((m, n), F32),
        scratch_shapes=[pltpu.VMEM((tm, k), BF16)],
        compiler_params=_params("parallel", "arbitrary"),
        name=name,
    )(*os_, *lses, w, x)


def _heads_res_kernel(a_ref, w_ref, x_ref, o_ref, as_ref, *, heads):
    @pl.when(pl.program_id(1) == 0)
    def _():
        for h in range(heads):
            as_ref[:, h * HEAD_DIM:(h + 1) * HEAD_DIM] = a_ref[h]

    o_ref[...] = x_ref[...] + jnp.dot(as_ref[...], w_ref[...], preferred_element_type=F32)


def _heads_res(a, w, x, *, tm=1024, tn=1024, name):
    heads, m, dh = a.shape
    k = heads * dh
    n = w.shape[1]
    return pl.pallas_call(
        functools.partial(_heads_res_kernel, heads=heads),
        grid=(m // tm, n // tn),
        in_specs=[
            pl.BlockSpec((heads, tm, dh), lambda i, j: (0, i, 0)),
            pl.BlockSpec((k, tn), lambda i, j: (0, j)),
            pl.BlockSpec((tm, tn), lambda i, j: (i, j)),
        ],
        out_specs=pl.BlockSpec((tm, tn), lambda i, j: (i, j)),
        out_shape=jax.ShapeDtypeStruct((m, n), F32),
        scratch_shapes=[pltpu.VMEM((tm, k), BF16)],
        compiler_params=_params("parallel", "arbitrary"),
        name=name,
    )(a, w, x)


def _band_attn_kernel(*refs, hq, hkv, half, qb, plen, run_p, run_s, has_sink, has_lse):
    it = iter(refs)
    sl_ref = next(it)
    sink_ref = next(it) if has_sink else None
    q_ref = next(it)
    kp_ref, kc_ref, kn_ref = next(it), next(it), next(it)
    vp_ref, vc_ref, vn_ref = next(it), next(it), next(it)
    o_ref = next(it)
    lse_ref = next(it) if has_lse else None

    g = hq // hkv
    nk = qb + 2 * half
    r0 = pl.program_id(0) * qb
    run_start, run_end = _run_bounds(r0, plen, run_p, run_s)
    qi = lax.broadcasted_iota(jnp.int32, (qb, nk), 0)
    kj = lax.broadcasted_iota(jnp.int32, (qb, nk), 1)
    dist = jnp.abs(qi + half - kj)
    kpos = r0 - half + kj
    valid = (dist <= half) & (kpos >= run_start) & (kpos < run_end)
    nbias = jnp.where(valid, -dist.astype(F32), -MASK_BIAS)

    for kv in range(hkv):
        cs = slice(kv * HEAD_DIM, (kv + 1) * HEAD_DIM)
        k = jnp.concatenate([kp_ref[:, cs], kc_ref[:, cs], kn_ref[:, cs]], axis=0)
        v = jnp.concatenate([vp_ref[:, cs], vc_ref[:, cs], vn_ref[:, cs]], axis=0)
        for gi in range(g):
            h = kv * g + gi
            hc = slice(h * HEAD_DIM, (h + 1) * HEAD_DIM)
            s = lax.dot_general(q_ref[:, hc], k, (((1,), (1,)), ((), ())),
                                preferred_element_type=F32)
            s = s + sl_ref[h] * nbias
            mx = jnp.max(s, axis=-1, keepdims=True)
            if has_sink:
                mx = jnp.maximum(mx, sink_ref[h])
            p = jnp.exp(s - mx)
            den = jnp.sum(p, axis=-1, keepdims=True)
            if has_sink:
                den = den + jnp.exp(sink_ref[h] - mx)
            o = jnp.dot(p.astype(BF16), v, preferred_element_type=F32) * (1.0 / den)
            o_ref[:, hc] = o.astype(o_ref.dtype)
            if has_lse:
                lse_ref[:, h:h + 1] = mx + jnp.log(den)


def _band_attn(q_arr, q_col0, k_arr, k_col0, v_arr, v_col0, slopes, sink, *, hq, hkv, half,
               plen, run_p, run_s, want_lse, name, qb=128):
    m = q_arr.shape[0]
    wq, wkv = hq * HEAD_DIM, hkv * HEAD_DIM
    ratio = qb // half
    nhb = m // half
    qc, kc, vc = q_col0 * HEAD_DIM // wq, k_col0 * HEAD_DIM // wkv, v_col0 * HEAD_DIM // wkv

    def prev_map(c):
        return lambda i: (jnp.maximum(i * ratio - 1, 0), c)

    def next_map(c):
        return lambda i: (jnp.minimum((i + 1) * ratio, nhb - 1), c)

    smem = pl.BlockSpec(memory_space=pltpu.SMEM)
    in_specs = [smem]
    args = [slopes]
    if sink is not None:
        in_specs.append(smem)
        args.append(sink)
    in_specs += [
        pl.BlockSpec((qb, wq), lambda i: (i, qc)),
        pl.BlockSpec((half, wkv), prev_map(kc)),
        pl.BlockSpec((qb, wkv), lambda i: (i, kc)),
        pl.BlockSpec((half, wkv), next_map(kc)),
        pl.BlockSpec((half, wkv), prev_map(vc)),
        pl.BlockSpec((qb, wkv), lambda i: (i, vc)),
        pl.BlockSpec((half, wkv), next_map(vc)),
    ]
    args += [q_arr, k_arr, k_arr, k_arr, v_arr, v_arr, v_arr]
    out_shape = [jax.ShapeDtypeStruct((m, wq), BF16)]
    out_specs = [pl.BlockSpec((qb, wq), lambda i: (i, 0))]
    if want_lse:
        out_shape.append(jax.ShapeDtypeStruct((m, hq), F32))
        out_specs.append(pl.BlockSpec((qb, hq), lambda i: (i, 0)))
    res = pl.pallas_call(
        functools.partial(_band_attn_kernel, hq=hq, hkv=hkv, half=half, qb=qb, plen=plen,
                          run_p=run_p, run_s=run_s, has_sink=sink is not None, has_lse=want_lse),
        grid=(m // qb,),
        in_specs=in_specs,
        out_specs=out_specs,
        out_shape=out_shape,
        compiler_params=_params("parallel"),
        name=name,
    )(*args)
    return res if want_lse else res[0]


def _rope128(x, cos_p, sin_a, sin_b):
    return x * cos_p + pltpu.roll(x, 96, 1) * sin_a + pltpu.roll(x, 32, 1) * sin_b


def _c_down_kernel(x_ref, g_ref, w_ref, qg_ref, kg_ref, cos_ref, sa_ref, sb_ref,
                   cq_ref, ckv_ref, kr_ref, hs_ref, *, tm):
    _rms_to_scratch(x_ref, g_ref, hs_ref, 0, tm)
    c = jnp.dot(hs_ref[...], w_ref[...], preferred_element_type=F32)
    cq_ref[...] = _rms_rows(c[:, :C_Q_RANK], qg_ref[...]).astype(cq_ref.dtype)
    ckv_ref[...] = _rms_rows(c[:, C_Q_RANK:C_Q_RANK + C_KV_RANK], kg_ref[...]).astype(ckv_ref.dtype)
    xr = c[:, C_Q_RANK + C_KV_RANK:]
    kr_ref[...] = _rope128(xr, cos_ref[...], sa_ref[...], sb_ref[...]).astype(kr_ref.dtype)


def _c_down(x, g, w_pad, qg, kg, tabs, *, tm=512):
    m, k = x.shape
    n = w_pad.shape[1]
    row = lambda w: pl.BlockSpec((tm, w), lambda i: (i, 0))
    full = lambda a, b: pl.BlockSpec((a, b), lambda i: (0, 0))
    return pl.pallas_call(
        functools.partial(_c_down_kernel, tm=tm),
        grid=(m // tm,),
        in_specs=[row(k), full(1, k), full(k, n), full(1, C_Q_RANK), full(1, C_KV_RANK),
                  row(128), row(128), row(128)],
        out_specs=[row(C_Q_RANK), row(C_KV_RANK), row(128)],
        out_shape=[jax.ShapeDtypeStruct((m, C_Q_RANK), BF16),
                   jax.ShapeDtypeStruct((m, C_KV_RANK), BF16),
                   jax.ShapeDtypeStruct((m, 128), BF16)],
        scratch_shapes=[pltpu.VMEM((tm, k), BF16)],
        compiler_params=_params("parallel"),
        name="c_down",
    )(x, g.reshape(1, k), w_pad, qg.reshape(1, -1), kg.reshape(1, -1), *tabs)


def _c_q_kernel(cq_ref, w_ref, cos_ref, sa_ref, sb_ref, q_ref, *, heads, scale):
    cq = cq_ref[...]
    cos_p, sin_a, sin_b = cos_ref[...], sa_ref[...], sb_ref[...]
    for h in range(heads):
        r = jnp.dot(cq, w_ref[:, h * 256:(h + 1) * 256], preferred_element_type=F32) * scale
        q_ref[h, :, :128] = r[:, :128].astype(q_ref.dtype)
        q_ref[h, :, 128:] = _rope128(r[:, 128:], cos_p, sin_a, sin_b).astype(q_ref.dtype)


def _c_q(cq, w, tabs, *, scale, tm=512):
    m, k = cq.shape
    heads = w.shape[1] // 256
    row = lambda w_: pl.BlockSpec((tm, w_), lambda i: (i, 0))
    return pl.pallas_call(
        functools.partial(_c_q_kernel, heads=heads, scale=scale),
        grid=(m // tm,),
        in_specs=[row(k), pl.BlockSpec(w.shape, lambda i: (0, 0)), row(128), row(128), row(128)],
        out_specs=pl.BlockSpec((heads, tm, 256), lambda i: (0, i, 0)),
        out_shape=jax.ShapeDtypeStruct((heads, m, 256), BF16),
        compiler_params=_params("parallel"),
        name="c_q_up",
    )(cq, w, *tabs)


def _c_kv_kernel(ckv_ref, w_ref, kr_ref, k_ref, v_ref, *, heads):
    ckv = ckv_ref[...]
    kr = kr_ref[...]
    for h in range(heads):
        r = jnp.dot(ckv, w_ref[:, h * 256:(h + 1) * 256], preferred_element_type=F32)
        k_ref[h, :, :128] = r[:, :128].astype(k_ref.dtype)
        k_ref[h, :, 128:] = kr
        v_ref[h] = r[:, 128:].astype(v_ref.dtype)


def _c_kv(ckv, w, kr, *, tm=512):
    m, k = ckv.shape
    heads = w.shape[1] // 256
    row = lambda w_: pl.BlockSpec((tm, w_), lambda i: (i, 0))
    return pl.pallas_call(
        functools.partial(_c_kv_kernel, heads=heads),
        grid=(m // tm,),
        in_specs=[row(k), pl.BlockSpec(w.shape, lambda i: (0, 0)), row(128)],
        out_specs=[pl.BlockSpec((heads, tm, 256), lambda i: (0, i, 0)),
                   pl.BlockSpec((heads, tm, 128), lambda i: (0, i, 0))],
        out_shape=[jax.ShapeDtypeStruct((heads, m, 256), BF16),
                   jax.ShapeDtypeStruct((heads, m, 128), BF16)],
        compiler_params=_params("parallel"),
        name="c_kv_up",
    )(ckv, w, kr)


def _flash_kernel(q_ref, k_ref, v_ref, o_ref, m_ref, l_ref, acc_ref, *, tk, nkt):
    m_ref[...] = jnp.full(m_ref.shape, -jnp.inf, F32)
    l_ref[...] = jnp.zeros(l_ref.shape, F32)
    acc_ref[...] = jnp.zeros(acc_ref.shape, F32)
    q = q_ref[...]

    def body(t, carry):
        r = pl.multiple_of(t * tk, tk)
        k = k_ref[pl.ds(r, tk), :]
        v = v_ref[pl.ds(r, tk), :]
        s = lax.dot_general(q, k, (((1,), (1,)), ((), ())), preferred_element_type=F32)
        m_prev = m_ref[...]
        m_new = jnp.maximum(m_prev, jnp.max(s, axis=-1, keepdims=True))
        a = jnp.exp(m_prev - m_new)
        p = jnp.exp(s - m_new)
        l_ref[...] = a * l_ref[...] + jnp.sum(p, axis=-1, keepdims=True)
        acc_ref[...] = a * acc_ref[...] + jnp.dot(p.astype(BF16), v, preferred_element_type=F32)
        m_ref[...] = m_new
        return carry

    lax.fori_loop(0, nkt, body, 0)
    o_ref[...] = (acc_ref[...] * (1.0 / l_ref[...])).astype(o_ref.dtype)


def _flash(q, k, v, *, row0, seq, nseq, tq=512, tk=512, name):
    heads, m, dq = q.shape
    dv = v.shape[2]
    qpt = seq // tq
    return pl.pallas_call(
        functools.partial(_flash_kernel, tk=tk, nkt=seq // tk),
        grid=(heads, nseq * qpt),
        in_specs=[
            pl.BlockSpec((None, tq, dq), lambda h, i: (h, row0 // tq + i, 0)),
            pl.BlockSpec((None, seq, dq), lambda h, i: (h, row0 // seq + i // qpt, 0)),
            pl.BlockSpec((None, seq, dv), lambda h, i: (h, row0 // seq + i // qpt, 0)),
        ],
        out_specs=pl.BlockSpec((None, tq, dv), lambda h, i: (h, i, 0)),
        out_shape=jax.ShapeDtypeStruct((heads, nseq * seq, dv), BF16),
        scratch_shapes=[pltpu.VMEM((tq, 1), F32), pltpu.VMEM((tq, 1), F32),
                        pltpu.VMEM((tq, dv), F32)],
        compiler_params=_params("parallel", "arbitrary"),
        name=name,
    )(q, k, v)


HALO = 16


def _ffn_kernel(x_ref, xp_ref, xn_ref, g_ref, wa_ref, wv_ref, cw_ref, cb_ref, wo_ref, o_ref,
                hs_ref, *, tm, plen, slen):
    i = pl.program_id(0)

    @pl.when(pl.program_id(1) == 0)
    def _():
        r0 = i * tm
        seq_start, seq_end = _run_bounds(r0, plen, plen, slen)
        g = g_ref[...]
        _rms_to_scratch(x_ref, g_ref, hs_ref, HALO, tm)
        hp = jnp.where(r0 > seq_start, _rms_rows(xp_ref[...], g), 0.0)
        hn = jnp.where(r0 + tm < seq_end, _rms_rows(xn_ref[...], g), 0.0)
        z = jnp.zeros_like(hp)
        hs_ref[0:HALO, :] = jnp.concatenate([z, hp], axis=0).astype(hs_ref.dtype)
        hs_ref[HALO + tm:2 * HALO + tm, :] = jnp.concatenate([hn, z], axis=0).astype(hs_ref.dtype)
        o_ref[...] = x_ref[...]

    rows = tm + 2 * HALO
    a = jnp.dot(hs_ref[...], wa_ref[...], preferred_element_type=F32)
    val = jnp.dot(hs_ref[HALO:HALO + tm, :], wv_ref[...], preferred_element_type=F32)
    a_prev = pltpu.roll(a, 1, 0)[HALO:HALO + tm]
    a_next = pltpu.roll(a, rows - 1, 0)[HALO:HALO + tm]
    cw = cw_ref[...]
    a = cw[0:1] * a_prev + cw[1:2] * a[HALO:HALO + tm] + cw[2:3] * a_next + cb_ref[...]
    gate = 0.5 * a * (1.0 + lax.erf(a * np.float32(np.sqrt(0.5))))
    o_ref[...] += jnp.dot((gate * val).astype(BF16), wo_ref[...], preferred_element_type=F32)


def _ffn(x, g, w_in, conv_w, conv_b, w_out, *, plen, slen, tm=512, tf=512, name):
    m, k = x.shape
    nf = D_FF // tf
    r8 = tm // 8
    nb8 = m // 8
    return pl.pallas_call(
        functools.partial(_ffn_kernel, tm=tm, plen=plen, slen=slen),
        grid=(m // tm, nf),
        in_specs=[
            pl.BlockSpec((tm, k), lambda i, j: (i, 0)),
            pl.BlockSpec((8, k), lambda i, j: (jnp.maximum(i * r8 - 1, 0), 0)),
            pl.BlockSpec((8, k), lambda i, j: (jnp.minimum((i + 1) * r8, nb8 - 1), 0)),
            pl.BlockSpec((1, k), lambda i, j: (0, 0)),
            pl.BlockSpec((k, tf), lambda i, j: (0, j)),
            pl.BlockSpec((k, tf), lambda i, j: (0, nf + j)),
            pl.BlockSpec((3, tf), lambda i, j: (0, j)),
            pl.BlockSpec((1, tf), lambda i, j: (0, j)),
            pl.BlockSpec((tf, k), lambda i, j: (j, 0)),
        ],
        out_specs=pl.BlockSpec((tm, k), lambda i, j: (i, 0)),
        out_shape=jax.ShapeDtypeStruct((m, k), F32),
        scratch_shapes=[pltpu.VMEM((tm + 2 * HALO, k), BF16)],
        compiler_params=_params("parallel", "arbitrary"),
        name=name,
    )(x, x, x, g.reshape(1, k), w_in, w_in, conv_w, conv_b.reshape(1, -1), w_out)


def _rmsnorm_kernel(x_ref, g_ref, o_ref):
    o_ref[...] = _rms_rows(x_ref[...], g_ref[...])


def _rmsnorm(x, g, *, row0, rows, tm=256, name):
    k = x.shape[1]
    return pl.pallas_call(
        _rmsnorm_kernel,
        grid=(rows // tm,),
        in_specs=[pl.BlockSpec((tm, k), lambda i: (row0 // tm + i, 0)),
                  pl.BlockSpec((1, k), lambda i: (0, 0))],
        out_specs=pl.BlockSpec((tm, k), lambda i: (i, 0)),
        out_shape=jax.ShapeDtypeStruct((rows, k), F32),
        compiler_params=_params("parallel"),
        name=name,
    )(x, g.reshape(1, k))


def _alibi_slopes(n):
    return jnp.power(2.0, -8.0 * jnp.arange(1, n + 1, dtype=F32) / n)


def _mixer_a(x, norm, w_qkv, sink, w_o, *, plen, slen, tag):
    nq, nk = A_HEADS * HEAD_DIM, A_KV_HEADS * HEAD_DIM
    colscale = jnp.concatenate([jnp.full((nq,), HEAD_DIM ** -0.5, F32), jnp.ones((2 * nk,), F32)])
    qkv = _norm_matmul(x, norm, w_qkv.astype(BF16), colscale, name=tag + "_qkv")
    o = _band_attn(qkv, 0, qkv, A_HEADS, qkv, A_HEADS + A_KV_HEADS, _alibi_slopes(A_HEADS), sink,
                   hq=A_HEADS, hkv=A_KV_HEADS, half=A_WINDOW, plen=plen, run_p=plen, run_s=slen,
                   want_lse=False, name=tag + "_attn")
    return _matmul_res(o, w_o.astype(BF16), x, name=tag + "_wo")


def _to_strided(t, d, plen, slen):
    if d == 1:
        return t
    m, c = t.shape
    p = t[:plen].reshape(plen // d, d, c).transpose(1, 0, 2).reshape(plen, c)
    ns = (m - plen) // slen
    s = t[plen:].reshape(ns, slen // d, d, c).transpose(0, 2, 1, 3).reshape(m - plen, c)
    return jnp.concatenate([p, s], axis=0)


def _from_strided(t, d, plen, slen):
    if d == 1:
        return t
    m, c = t.shape
    p = t[:plen].reshape(d, plen // d, c).transpose(1, 0, 2).reshape(plen, c)
    ns = (m - plen) // slen
    s = t[plen:].reshape(ns, d, slen // d, c).transpose(0, 2, 1, 3).reshape(m - plen, c)
    return jnp.concatenate([p, s], axis=0)


def _mixer_b(x, norm, w_qkv, w_o, *, plen, slen, tag):
    hw = B_HEADS * HEAD_DIM
    colscale = jnp.tile(
        jnp.concatenate([jnp.full((hw,), HEAD_DIM ** -0.5, F32), jnp.ones((2 * hw,), F32)]),
        len(B_GROUPS))
    qkv = _norm_matmul(x, norm, w_qkv.astype(BF16), colscale, name=tag + "_qkv")
    outs, lses = [], []
    for gi, (window, dil) in enumerate(B_GROUPS):
        half = (window // 2) // dil
        if dil == 1:
            grp, base = qkv, gi * 3 * B_HEADS
        else:
            grp, base = _to_strided(qkv[:, gi * 3 * hw:(gi + 1) * 3 * hw], dil, plen, slen), 0
        o, lse = _band_attn(grp, base, grp, base + B_HEADS, grp, base + 2 * B_HEADS,
                            _alibi_slopes(B_HEADS) * dil, None,
                            hq=B_HEADS, hkv=B_HEADS, half=half, plen=plen,
                            run_p=plen // dil, run_s=slen // dil, want_lse=True,
                            name="%s_attn_d%d" % (tag, dil))
        outs.append(_from_strided(o, dil, plen, slen))
        lses.append(_from_strided(lse, dil, plen, slen))
    return _merge_res(outs, lses, w_o.astype(BF16), x, name=tag + "_wo")


def _rope_tabs(plen, slen, nsample):
    pos = jnp.arange(plen, dtype=F32)
    inv = jnp.power(ROPE_THETA, -jnp.arange(0, C_ROPE, 2, dtype=F32) / C_ROPE)
    ang = pos[:, None] * inv[None, :]
    cos, sin = jnp.cos(ang), jnp.sin(ang)
    flat = lambda t: jnp.concatenate([t] + [t[:slen]] * nsample, axis=0)
    cos, sin = flat(cos), flat(sin)
    z32, z64 = jnp.zeros_like(cos), jnp.zeros((cos.shape[0], 64), F32)
    return (jnp.concatenate([cos, cos, z64], axis=1),
            jnp.concatenate([-sin, z32, z64], axis=1),
            jnp.concatenate([z32, sin, z64], axis=1))


def _mixer_c(x, norm, w_down, q_norm, kv_norm, w_uq, w_ukv, w_o, *, plen, slen, tag):
    m = x.shape[0]
    nsample = (m - plen) // slen
    tabs = _rope_tabs(plen, slen, nsample)
    w_down_p = jnp.pad(w_down, ((0, 0), (0, 128 - C_ROPE))).astype(BF16)
    cq, ckv, kr = _c_down(x, norm, w_down_p, q_norm, kv_norm, tabs)
    wq = w_uq.reshape(C_Q_RANK, C_HEADS, C_NOPE + C_ROPE)
    wq = jnp.pad(wq, ((0, 0), (0, 0), (0, 256 - C_NOPE - C_ROPE))).reshape(C_Q_RANK, C_HEADS * 256)
    q = _c_q(cq, wq.astype(BF16), tabs, scale=(C_NOPE + C_ROPE) ** -0.5)
    k, v = _c_kv(ckv, w_ukv.astype(BF16), kr)
    o_p = _flash(q, k, v, row0=0, seq=plen, nseq=1, name=tag + "_flash_p")
    o_s = _flash(q, k, v, row0=plen, seq=slen, nseq=nsample, name=tag + "_flash_s")
    o = jnp.concatenate([o_p, o_s], axis=1)
    return _heads_res(o, w_o.astype(BF16), x, name=tag + "_wo")


def kernel(x_prompt, x_sample, l0_mix_norm, l0_a_w_qkv, l0_a_sink, l0_a_w_o, l0_ffn_norm, l0_ffn_w_in, l0_ffn_conv_w, l0_ffn_conv_b, l0_ffn_w_out, l1_mix_norm, l1_b_w_qkv, l1_b_w_o, l1_ffn_norm, l1_ffn_w_in, l1_ffn_conv_w, l1_ffn_conv_b, l1_ffn_w_out, l2_mix_norm, l2_c_w_down, l2_c_q_norm, l2_c_kv_norm, l2_c_w_uq, l2_c_w_ukv, l2_c_w_o, l2_ffn_norm, l2_ffn_w_in, l2_ffn_conv_w, l2_ffn_conv_b, l2_ffn_w_out, l3_mix_norm, l3_a_w_qkv, l3_a_sink, l3_a_w_o, l3_ffn_norm, l3_ffn_w_in, l3_ffn_conv_w, l3_ffn_conv_b, l3_ffn_w_out, final_norm):
    bp, sp, d = x_prompt.shape
    bs, ss, _ = x_sample.shape
    plen = bp * sp
    assert bp == 1, "prompt rows are treated as one sequence"
    x = jnp.concatenate([x_prompt.reshape(plen, d), x_sample.reshape(bs * ss, d)], axis=0)
    kw = dict(plen=plen, slen=ss)

    def ffn(x, norm, w_in, conv_w, conv_b, w_out, tag):
        return _ffn(x, norm, w_in.astype(BF16), conv_w, conv_b, w_out.astype(BF16), name=tag, **kw)

    x = _mixer_a(x, l0_mix_norm, l0_a_w_qkv, l0_a_sink, l0_a_w_o, tag="l0_a", **kw)
    x = ffn(x, l0_ffn_norm, l0_ffn_w_in, l0_ffn_conv_w, l0_ffn_conv_b, l0_ffn_w_out, "l0_ffn")
    x = _mixer_b(x, l1_mix_norm, l1_b_w_qkv, l1_b_w_o, tag="l1_b", **kw)
    x = ffn(x, l1_ffn_norm, l1_ffn_w_in, l1_ffn_conv_w, l1_ffn_conv_b, l1_ffn_w_out, "l1_ffn")
    x = _mixer_c(x, l2_mix_norm, l2_c_w_down, l2_c_q_norm, l2_c_kv_norm, l2_c_w_uq, l2_c_w_ukv,
                 l2_c_w_o, tag="l2_c", **kw)
    x = ffn(x, l2_ffn_norm, l2_ffn_w_in, l2_ffn_conv_w, l2_ffn_conv_b, l2_ffn_w_out, "l2_ffn")
    x = _mixer_a(x, l3_mix_norm, l3_a_w_qkv, l3_a_sink, l3_a_w_o, tag="l3_a", **kw)
    x = ffn(x, l3_ffn_norm, l3_ffn_w_in, l3_ffn_conv_w, l3_ffn_conv_b, l3_ffn_w_out, "l3_ffn")
    y_p = _rmsnorm(x, final_norm, row0=0, rows=plen, name="final_norm_p")
    y_s = _rmsnorm(x, final_norm, row0=plen, rows=bs * ss, name="final_norm_s")
    return (y_p.reshape(bp, sp, d), y_s.reshape(bs, ss, d))
```

```python
import functools

import jax
import jax.numpy as jnp
import numpy as np
from jax import lax
from jax.experimental import pallas as pl
from jax.experimental.pallas import tpu as pltpu

F32 = jnp.float32
BF16 = jnp.bfloat16

D_MODEL = 2048
HEAD_DIM = 128
NORM_EPS = 1e-6
A_HEADS = 16
A_KV_HEADS = 4
A_WINDOW = 128
B_HEADS = 16
B_GROUPS = ((128, 1), (512, 4), (2048, 16))
C_HEADS = 16
C_Q_RANK = 512
C_KV_RANK = 512
C_NOPE = 128
C_ROPE = 64
C_V = 128
ROPE_THETA = 10000.0
D_FF = 5632

VMEM_LIMIT_BYTES = 56 * 1024 * 1024
LOG2E = float(np.log2(np.e))
FLASH_TK = 512
MASK_BIAS = 1e30


def _params(*sem):
    return pltpu.CompilerParams(dimension_semantics=sem, vmem_limit_bytes=VMEM_LIMIT_BYTES)


def _run_bounds(r0, plen, run_p, run_s):
    in_p = r0 < plen
    start = jnp.where(in_p, (r0 // run_p) * run_p, plen + ((r0 - plen) // run_s) * run_s)
    end = start + jnp.where(in_p, run_p, run_s)
    return start, end


def _rms_rows(x, g):
    ms = jnp.mean(x * x, axis=-1, keepdims=True)
    return (x * lax.rsqrt(ms + NORM_EPS)) * g


def _rms_to_scratch(x_ref, g_ref, hs_ref, row_off, nrows, chunk=64):
    g = g_ref[...]

    def body(c, carry):
        r = pl.multiple_of(c * chunk, chunk)
        y = _rms_rows(x_ref[pl.ds(r, chunk), :], g)
        hs_ref[pl.ds(row_off + r, chunk), :] = y.astype(hs_ref.dtype)
        return carry

    lax.fori_loop(0, nrows // chunk, body, 0)


def _norm_matmul_kernel(x_ref, g_ref, w_ref, cs_ref, o_ref, hs_ref, *, tm):
    @pl.when(pl.program_id(1) == 0)
    def _():
        _rms_to_scratch(x_ref, g_ref, hs_ref, 0, tm)

    acc = jnp.dot(hs_ref[...], w_ref[...], preferred_element_type=F32)
    o_ref[...] = (acc * cs_ref[...]).astype(o_ref.dtype)


def _norm_matmul(x, g, w, colscale, *, tm=1024, tn=1024, name):
    m, k = x.shape
    n = w.shape[1]
    return pl.pallas_call(
        functools.partial(_norm_matmul_kernel, tm=tm),
        grid=(m // tm, n // tn),
        in_specs=[
            pl.BlockSpec((tm, k), lambda i, j: (i, 0)),
            pl.BlockSpec((1, k), lambda i, j: (0, 0)),
            pl.BlockSpec((k, tn), lambda i, j: (0, j)),
            pl.BlockSpec((1, tn), lambda i, j: (0, j)),
        ],
        out_specs=pl.BlockSpec((tm, tn), lambda i, j: (i, j)),
        out_shape=jax.ShapeDtypeStruct((m, n), BF16),
        scratch_shapes=[pltpu.VMEM((tm, k), BF16)],
        compiler_params=_params("parallel", "arbitrary"),
        name=name,
    )(x, g.reshape(1, k), w, colscale.reshape(1, n))


def _matmul_res_kernel(a_ref, w_ref, x_ref, o_ref):
    o_ref[...] = x_ref[...] + jnp.dot(a_ref[...], w_ref[...], preferred_element_type=F32)


def _matmul_res(a, w, x, *, tm=1024, tn=1024, name):
    m, k = a.shape
    n = w.shape[1]
    return pl.pallas_call(
        _matmul_res_kernel,
        grid=(m // tm, n // tn),
        in_specs=[
            pl.BlockSpec((tm, k), lambda i, j: (i, 0)),
            pl.BlockSpec((k, tn), lambda i, j: (0, j)),
            pl.BlockSpec((tm, tn), lambda i, j: (i, j)),
        ],
        out_specs=pl.BlockSpec((tm, tn), lambda i, j: (i, j)),
        out_shape=jax.ShapeDtypeStruct((m, n), F32),
        compiler_params=_params("parallel", "arbitrary"),
        name=name,
    )(a, w, x)


def _merge_res_kernel(o1_ref, o2_ref, o3_ref, l1_ref, l2_ref, l3_ref, w_ref, x_ref, o_ref, a_ref,
                      *, heads):
    @pl.when(pl.program_id(1) == 0)
    def _():
        l1, l2, l3 = l1_ref[...], l2_ref[...], l3_ref[...]
        mx = jnp.maximum(jnp.maximum(l1, l2), l3)
        e1, e2, e3 = jnp.exp(l1 - mx), jnp.exp(l2 - mx), jnp.exp(l3 - mx)
        inv = 1.0 / (e1 + e2 + e3)
        a1, a2, a3 = e1 * inv, e2 * inv, e3 * inv
        for h in range(heads):
            cs = slice(h * HEAD_DIM, (h + 1) * HEAD_DIM)
            hs = slice(h, h + 1)
            acc = (a1[:, hs] * o1_ref[:, cs].astype(F32)
                   + a2[:, hs] * o2_ref[:, cs].astype(F32)
                   + a3[:, hs] * o3_ref[:, cs].astype(F32))
            a_ref[:, cs] = acc.astype(a_ref.dtype)

    o_ref[...] = x_ref[...] + jnp.dot(a_ref[...], w_ref[...], preferred_element_type=F32)


def _merge_res(os_, lses, w, x, *, tm=512, tn=1024, name):
    m, k = os_[0].shape
    n = w.shape[1]
    heads = lses[0].shape[1]
    o_spec = pl.BlockSpec((tm, k), lambda i, j: (i, 0))
    l_spec = pl.BlockSpec((tm, heads), lambda i, j: (i, 0))
    return pl.pallas_call(
        functools.partial(_merge_res_kernel, heads=heads),
        grid=(m // tm, n // tn),
        in_specs=[o_spec, o_spec, o_spec, l_spec, l_spec, l_spec,
                  pl.BlockSpec((k, tn), lambda i, j: (0, j)),
                  pl.BlockSpec((tm, tn), lambda i, j: (i, j))],
        out_specs=pl.BlockSpec((tm, tn), lambda i, j: (i, j)),
        out_shape=jax.ShapeDtypeStruct((m, n), F32),
        scratch_shapes=[pltpu.VMEM((tm, k), BF16)],
        compiler_params=_params("parallel", "arbitrary"),
        name=name,
    )(*os_, *lses, w, x)


def _heads_res_kernel(a_ref, w_ref, x_ref, o_ref, as_ref, *, heads):
    @pl.when(pl.program_id(1) == 0)
    def _():
        for h in range(heads):
            as_ref[:, h * HEAD_DIM:(h + 1) * HEAD_DIM] = a_ref[h]

    o_ref[...] = x_ref[...] + jnp.dot(as_ref[...], w_ref[...], preferred_element_type=F32)


def _heads_res(a, w, x, *, tm=1024, tn=1024, name):
    heads, m, dh = a.shape
    k = heads * dh
    n = w.shape[1]
    return pl.pallas_call(
        functools.partial(_heads_res_kernel, heads=heads),
        grid=(m // tm, n // tn),
        in_specs=[
            pl.BlockSpec((heads, tm, dh), lambda i, j: (0, i, 0)),
            pl.BlockSpec((k, tn), lambda i, j: (0, j)),
            pl.BlockSpec((tm, tn), lambda i, j: (i, j)),
        ],
        out_specs=pl.BlockSpec((tm, tn), lambda i, j: (i, j)),
        out_shape=jax.ShapeDtypeStruct((m, n), F32),
        scratch_shapes=[pltpu.VMEM((tm, k), BF16)],
        compiler_params=_params("parallel", "arbitrary"),
        name=name,
    )(a, w, x)


def _band_attn_kernel(*refs, hq, hkv, half, qb, plen, run_p, run_s, has_sink, has_lse):
    it = iter(refs)
    sl_ref = next(it)
    sink_ref = next(it) if has_sink else None
    q_ref = next(it)
    kp_ref, kc_ref, kn_ref = next(it), next(it), next(it)
    vp_ref, vc_ref, vn_ref = next(it), next(it), next(it)
    o_ref = next(it)
    lse_ref = next(it) if has_lse else None

    g = hq // hkv
    nk = qb + 2 * half
    r0 = pl.program_id(0) * qb
    run_start, run_end = _run_bounds(r0, plen, run_p, run_s)
    qi = lax.broadcasted_iota(jnp.int32, (qb, nk), 0)
    kj = lax.broadcasted_iota(jnp.int32, (qb, nk), 1)
    dist = jnp.abs(qi + half - kj)
    kpos = r0 - half + kj
    valid = (dist <= half) & (kpos >= run_start) & (kpos < run_end)
    nbias = jnp.where(valid, -dist.astype(F32), -MASK_BIAS)

    for kv in range(hkv):
        cs = slice(kv * HEAD_DIM, (kv + 1) * HEAD_DIM)
        k = jnp.concatenate([kp_ref[:, cs], kc_ref[:, cs], kn_ref[:, cs]], axis=0)
        v = jnp.concatenate([vp_ref[:, cs], vc_ref[:, cs], vn_ref[:, cs]], axis=0)
        for gi in range(g):
            h = kv * g + gi
            hc = slice(h * HEAD_DIM, (h + 1) * HEAD_DIM)
            s = lax.dot_general(q_ref[:, hc], k, (((1,), (1,)), ((), ())),
                                preferred_element_type=F32)
            s = s + sl_ref[h] * nbias
            mx = jnp.max(s, axis=-1, keepdims=True)
            if has_sink:
                mx = jnp.maximum(mx, sink_ref[h])
            p = jnp.exp(s - mx)
            den = jnp.sum(p, axis=-1, keepdims=True)
            if has_sink:
                den = den + jnp.exp(sink_ref[h] - mx)
            o = jnp.dot(p.astype(BF16), v, preferred_element_type=F32) * (1.0 / den)
            o_ref[:, hc] = o.astype(o_ref.dtype)
            if has_lse:
                lse_ref[:, h:h + 1] = mx + jnp.log(den)


def _band_attn(q_arr, q_col0, k_arr, k_col0, v_arr, v_col0, slopes, sink, *, hq, hkv, half,
               plen, run_p, run_s, want_lse, name, qb=128):
    m = q_arr.shape[0]
    wq, wkv = hq * HEAD_DIM, hkv * HEAD_DIM
    ratio = qb // half
    nhb = m // half
    qc, kc, vc = q_col0 * HEAD_DIM // wq, k_col0 * HEAD_DIM // wkv, v_col0 * HEAD_DIM // wkv

    def prev_map(c):
        return lambda i: (jnp.maximum(i * ratio - 1, 0), c)

    def next_map(c):
        return lambda i: (jnp.minimum((i + 1) * ratio, nhb - 1), c)

    smem = pl.BlockSpec(memory_space=pltpu.SMEM)
    in_specs = [smem]
    args = [slopes]
    if sink is not None:
        in_specs.append(smem)
        args.append(sink)
    in_specs += [
        pl.BlockSpec((qb, wq), lambda i: (i, qc)),
        pl.BlockSpec((half, wkv), prev_map(kc)),
        pl.BlockSpec((qb, wkv), lambda i: (i, kc)),
        pl.BlockSpec((half, wkv), next_map(kc)),
        pl.BlockSpec((half, wkv), prev_map(vc)),
        pl.BlockSpec((qb, wkv), lambda i: (i, vc)),
        pl.BlockSpec((half, wkv), next_map(vc)),
    ]
    args += [q_arr, k_arr, k_arr, k_arr, v_arr, v_arr, v_arr]
    out_shape = [jax.ShapeDtypeStruct((m, wq), BF16)]
    out_specs = [pl.BlockSpec((qb, wq), lambda i: (i, 0))]
    if want_lse:
        out_shape.append(jax.ShapeDtypeStruct((m, hq), F32))
        out_specs.append(pl.BlockSpec((qb, hq), lambda i: (i, 0)))
    res = pl.pallas_call(
        functools.partial(_band_attn_kernel, hq=hq, hkv=hkv, half=half, qb=qb, plen=plen,
                          run_p=run_p, run_s=run_s, has_sink=sink is not None, has_lse=want_lse),
        grid=(m // qb,),
        in_specs=in_specs,
        out_specs=out_specs,
        out_shape=out_shape,
        compiler_params=_params("parallel"),
        name=name,
    )(*args)
    return res if want_lse else res[0]


def _rope128(x, cos_p, sin_a, sin_b):
    return x * cos_p + pltpu.roll(x, 96, 1) * sin_a + pltpu.roll(x, 32, 1) * sin_b


def _c_down_kernel(x_ref, g_ref, w_ref, qg_ref, kg_ref, cos_ref, sa_ref, sb_ref,
                   cq_ref, ckv_ref, kr_ref, hs_ref, *, tm):
    _rms_to_scratch(x_ref, g_ref, hs_ref, 0, tm)
    c = jnp.dot(hs_ref[...], w_ref[...], preferred_element_type=F32)
    cq_ref[...] = _rms_rows(c[:, :C_Q_RANK], qg_ref[...]).astype(cq_ref.dtype)
    ckv_ref[...] = _rms_rows(c[:, C_Q_RANK:C_Q_RANK + C_KV_RANK], kg_ref[...]).astype(ckv_ref.dtype)
    xr = c[:, C_Q_RANK + C_KV_RANK:]
    kr_ref[...] = _rope128(xr, cos_ref[...], sa_ref[...], sb_ref[...]).astype(kr_ref.dtype)


def _c_down(x, g, w_pad, qg, kg, tabs, *, tm=512):
    m, k = x.shape
    n = w_pad.shape[1]
    row = lambda w: pl.BlockSpec((tm, w), lambda i: (i, 0))
    full = lambda a, b: pl.BlockSpec((a, b), lambda i: (0, 0))
    return pl.pallas_call(
        functools.partial(_c_down_kernel, tm=tm),
        grid=(m // tm,),
        in_specs=[row(k), full(1, k), full(k, n), full(1, C_Q_RANK), full(1, C_KV_RANK),
                  row(128), row(128), row(128)],
        out_specs=[row(C_Q_RANK), row(C_KV_RANK), row(128)],
        out_shape=[jax.ShapeDtypeStruct((m, C_Q_RANK), BF16),
                   jax.ShapeDtypeStruct((m, C_KV_RANK), BF16),
                   jax.ShapeDtypeStruct((m, 128), BF16)],
        scratch_shapes=[pltpu.VMEM((tm, k), BF16)],
        compiler_params=_params("parallel"),
        name="c_down",
    )(x, g.reshape(1, k), w_pad, qg.reshape(1, -1), kg.reshape(1, -1), *tabs)


def _c_q_kernel(cq_ref, w_ref, cos_ref, sa_ref, sb_ref, q_ref, *, heads, scale):
    cq = cq_ref[...]
    cos_p, sin_a, sin_b = cos_ref[...], sa_ref[...], sb_ref[...]
    for h in range(heads):
        r = jnp.dot(cq, w_ref[:, h * 256:(h + 1) * 256], preferred_element_type=F32) * scale
        q_ref[h, :, :128] = r[:, :128].astype(q_ref.dtype)
        q_ref[h, :, 128:] = _rope128(r[:, 128:], cos_p, sin_a, sin_b).astype(q_ref.dtype)


def _c_q(cq, w, tabs, *, scale, tm=512):
    m, k = cq.shape
    heads = w.shape[1] // 256
    row = lambda w_: pl.BlockSpec((tm, w_), lambda i: (i, 0))
    return pl.pallas_call(
        functools.partial(_c_q_kernel, heads=heads, scale=scale),
        grid=(m // tm,),
        in_specs=[row(k), pl.BlockSpec(w.shape, lambda i: (0, 0)), row(128), row(128), row(128)],
        out_specs=pl.BlockSpec((heads, tm, 256), lambda i: (0, i, 0)),
        out_shape=jax.ShapeDtypeStruct((heads, m, 256), BF16),
        compiler_params=_params("parallel"),
        name="c_q_up",
    )(cq, w, *tabs)


def _c_kv_kernel(ckv_ref, w_ref, kr_ref, k_ref, vt_ref, *, heads):
    ckv = ckv_ref[...]
    kr = kr_ref[...]
    for h in range(heads):
        r = jnp.dot(ckv, w_ref[:, h * 256:(h + 1) * 256], preferred_element_type=F32)
        k_ref[h, :, :128] = r[:, :128].astype(k_ref.dtype)
        k_ref[h, :, 128:] = kr
        vt_ref[h, 0] = r[:, 128:].T.astype(vt_ref.dtype)


def _c_kv(ckv, w, kr, *, tm):
    m, k = ckv.shape
    heads = w.shape[1] // 256
    row = lambda w_: pl.BlockSpec((tm, w_), lambda i: (i, 0))
    return pl.pallas_call(
        functools.partial(_c_kv_kernel, heads=heads),
        grid=(m // tm,),
        in_specs=[row(k), pl.BlockSpec(w.shape, lambda i: (0, 0)), row(128)],
        out_specs=[pl.BlockSpec((heads, tm, 256), lambda i: (0, i, 0)),
                   pl.BlockSpec((heads, 1, C_V, tm), lambda i: (0, i, 0, 0))],
        out_shape=[jax.ShapeDtypeStruct((heads, m, 256), BF16),
                   jax.ShapeDtypeStruct((heads, m // tm, C_V, tm), BF16)],
        compiler_params=_params("parallel"),
        name="c_kv_up",
    )(ckv, w, kr)


def _flash_kernel(*refs, tk, nkt, nchain, aliased):
    q_ref, k_ref, vt_ref = refs[0], refs[1], refs[2]
    o_ref, sa_ref, sb_ref = refs[4:7] if aliased else refs[3:6]
    assert nkt % 2 == 0
    tq = q_ref.shape[0]
    tc = tq // nchain
    dv = vt_ref.shape[1]
    qs = [q_ref[c * tc:(c + 1) * tc, :] for c in range(nchain)]

    def scores(t, dst_ref):
        k = k_ref[pl.ds(pl.multiple_of(t * tk, tk), tk), :]
        for c in range(nchain):
            dst_ref[c] = lax.dot_general(k, qs[c], (((1,), (1,)), ((), ())),
                                         preferred_element_type=F32)

    def update(t, src_ref, states):
        out = []
        for c in range(nchain):
            m_prev, l_prev, acc = states[c]
            m_new = jnp.maximum(m_prev, jnp.max(src_ref[c], axis=0, keepdims=True))
            a = jnp.exp2(m_prev - m_new)
            pt = jnp.exp2(src_ref[c] - m_new)
            l_new = a * l_prev + jnp.sum(pt, axis=0, keepdims=True)
            acc = a * acc + jnp.dot(vt_ref[t], pt.astype(BF16), preferred_element_type=F32)
            out.append((m_new, l_new, acc))
        return tuple(out)

    def body(u, states):
        t = 2 * u
        scores(t + 1, sb_ref)
        states = update(t, sa_ref, states)
        scores(t + 2, sa_ref)
        return update(t + 1, sb_ref, states)

    states = tuple((jnp.full((1, tc), -jnp.inf, F32), jnp.zeros((1, tc), F32),
                    jnp.zeros((dv, tc), F32)) for _ in range(nchain))
    scores(0, sa_ref)
    states = lax.fori_loop(0, nkt // 2 - 1, body, states)
    scores(nkt - 1, sb_ref)
    states = update(nkt - 2, sa_ref, states)
    states = update(nkt - 1, sb_ref, states)
    for c in range(nchain):
        _, l_fin, acc = states[c]
        o_ref[c * tc:(c + 1) * tc, :] = (acc * (1.0 / l_fin)).T.astype(o_ref.dtype)


def _flash(q, k, vt, prev_out, *, row0, seq, nseq, tq=512, nchain=2, name):
    heads, m, dq = q.shape
    _, _, dv, tk = vt.shape
    qpt = seq // tq
    kpt = seq // tk
    in_specs = [
        pl.BlockSpec((None, tq, dq), lambda h, i: (h, row0 // tq + i, 0)),
        pl.BlockSpec((None, seq, dq), lambda h, i: (h, row0 // seq + i // qpt, 0)),
        pl.BlockSpec((None, kpt, dv, tk), lambda h, i: (h, row0 // seq + i // qpt, 0, 0)),
    ]
    args = [q, k, vt]
    aliases = {}
    if prev_out is not None:
        in_specs.append(pl.BlockSpec(memory_space=pl.ANY))
        args.append(prev_out)
        aliases = {3: 0}
    return pl.pallas_call(
        functools.partial(_flash_kernel, tk=tk, nkt=kpt, nchain=nchain,
                          aliased=prev_out is not None),
        grid=(heads, nseq * qpt),
        in_specs=in_specs,
        out_specs=pl.BlockSpec((None, tq, dv), lambda h, i: (h, row0 // tq + i, 0)),
        out_shape=jax.ShapeDtypeStruct((heads, m, dv), BF16),
        scratch_shapes=[pltpu.VMEM((nchain, tk, tq // nchain), F32)] * 2,
        input_output_aliases=aliases,
        compiler_params=_params("parallel", "arbitrary"),
        name=name,
    )(*args)


HALO = 16


def _ffn_kernel(x_ref, xp_ref, xn_ref, g_ref, wa_ref, wv_ref, cw_ref, cb_ref, wo_ref, o_ref,
                hs_ref, ga_ref, gb_ref, *, tm, nf, plen, slen):
    i = pl.program_id(0)
    j = pl.program_id(1)
    rows = tm + 2 * HALO

    @pl.when(j == 0)
    def _():
        r0 = i * tm
        seq_start, seq_end = _run_bounds(r0, plen, plen, slen)
        g = g_ref[...]
        _rms_to_scratch(x_ref, g_ref, hs_ref, HALO, tm)
        hp = jnp.where(r0 > seq_start, _rms_rows(xp_ref[...], g), 0.0)
        hn = jnp.where(r0 + tm < seq_end, _rms_rows(xn_ref[...], g), 0.0)
        z = jnp.zeros_like(hp)
        hs_ref[0:HALO, :] = jnp.concatenate([z, hp], axis=0).astype(hs_ref.dtype)
        hs_ref[HALO + tm:2 * HALO + tm, :] = jnp.concatenate([hn, z], axis=0).astype(hs_ref.dtype)
        o_ref[...] = x_ref[...]

    def step(dst_ref, src_ref):
        if dst_ref is not None:
            a = jnp.dot(hs_ref[...], wa_ref[...], preferred_element_type=F32)
            val = jnp.dot(hs_ref[HALO:HALO + tm, :], wv_ref[...], preferred_element_type=F32)
        if src_ref is not None:
            o_ref[...] += jnp.dot(src_ref[...], wo_ref[...], preferred_element_type=F32)
        if dst_ref is not None:
            a_prev = pltpu.roll(a, 1, 0)[HALO:HALO + tm]
            a_next = pltpu.roll(a, rows - 1, 0)[HALO:HALO + tm]
            cw = cw_ref[...]
            a = cw[0:1] * a_prev + cw[1:2] * a[HALO:HALO + tm] + cw[2:3] * a_next + cb_ref[...]
            gate = 0.5 * a * (1.0 + lax.erf(a * np.float32(np.sqrt(0.5))))
            dst_ref[...] = (gate * val).astype(dst_ref.dtype)

    inner = (j > 0) & (j < nf)
    pl.when(j == 0)(lambda: step(ga_ref, None))
    pl.when(inner & (j % 2 == 1))(lambda: step(gb_ref, ga_ref))
    pl.when(inner & (j % 2 == 0))(lambda: step(ga_ref, gb_ref))
    pl.when(j == nf)(lambda: step(None, gb_ref if nf % 2 == 0 else ga_ref))


def _ffn(x, g, w_in, conv_w, conv_b, w_out, *, plen, slen, tm=512, tf=512, name):
    m, k = x.shape
    nf = D_FF // tf
    r8 = tm // 8
    nb8 = m // 8
    cur = lambda j: jnp.minimum(j, nf - 1)
    return pl.pallas_call(
        functools.partial(_ffn_kernel, tm=tm, nf=nf, plen=plen, slen=slen),
        grid=(m // tm, nf + 1),
        in_specs=[
            pl.BlockSpec((tm, k), lambda i, j: (i, 0)),
            pl.BlockSpec((8, k), lambda i, j: (jnp.maximum(i * r8 - 1, 0), 0)),
            pl.BlockSpec((8, k), lambda i, j: (jnp.minimum((i + 1) * r8, nb8 - 1), 0)),
            pl.BlockSpec((1, k), lambda i, j: (0, 0)),
            pl.BlockSpec((k, tf), lambda i, j: (0, cur(j))),
            pl.BlockSpec((k, tf), lambda i, j: (0, nf + cur(j))),
            pl.BlockSpec((3, tf), lambda i, j: (0, cur(j))),
            pl.BlockSpec((1, tf), lambda i, j: (0, cur(j))),
            pl.BlockSpec((tf, k), lambda i, j: (jnp.maximum(j - 1, 0), 0)),
        ],
        out_specs=pl.BlockSpec((tm, k), lambda i, j: (i, 0)),
        out_shape=jax.ShapeDtypeStruct((m, k), F32),
        scratch_shapes=[pltpu.VMEM((tm + 2 * HALO, k), BF16),
                        pltpu.VMEM((tm, tf), BF16), pltpu.VMEM((tm, tf), BF16)],
        compiler_params=_params("parallel", "arbitrary"),
        name=name,
    )(x, x, x, g.reshape(1, k), w_in, w_in, conv_w, conv_b.reshape(1, -1), w_out)


def _rmsnorm_kernel(x_ref, g_ref, o_ref):
    o_ref[...] = _rms_rows(x_ref[...], g_ref[...])


def _rmsnorm(x, g, *, row0, rows, tm=256, name):
    k = x.shape[1]
    return pl.pallas_call(
        _rmsnorm_kernel,
        grid=(rows // tm,),
        in_specs=[pl.BlockSpec((tm, k), lambda i: (row0 // tm + i, 0)),
                  pl.BlockSpec((1, k), lambda i: (0, 0))],
        out_specs=pl.BlockSpec((tm, k), lambda i: (i, 0)),
        out_shape=jax.ShapeDtypeStruct((rows, k), F32),
        compiler_params=_params("parallel"),
        name=name,
    )(x, g.reshape(1, k))


def _alibi_slopes(n):
    return jnp.power(2.0, -8.0 * jnp.arange(1, n + 1, dtype=F32) / n)


def _mixer_a(x, norm, w_qkv, sink, w_o, *, plen, slen, tag):
    nq, nk = A_HEADS * HEAD_DIM, A_KV_HEADS * HEAD_DIM
    colscale = jnp.concatenate([jnp.full((nq,), HEAD_DIM ** -0.5, F32), jnp.ones((2 * nk,), F32)])
    qkv = _norm_matmul(x, norm, w_qkv.astype(BF16), colscale, name=tag + "_qkv")
    o = _band_attn(qkv, 0, qkv, A_HEADS, qkv, A_HEADS + A_KV_HEADS, _alibi_slopes(A_HEADS), sink,
                   hq=A_HEADS, hkv=A_KV_HEADS, half=A_WINDOW, plen=plen, run_p=plen, run_s=slen,
                   want_lse=False, name=tag + "_attn")
    return _matmul_res(o, w_o.astype(BF16), x, name=tag + "_wo")


def _to_strided(t, d, plen, slen):
    if d == 1:
        return t
    m, c = t.shape
    p = t[:plen].reshape(plen // d, d, c).transpose(1, 0, 2).reshape(plen, c)
    ns = (m - plen) // slen
    s = t[plen:].reshape(ns, slen // d, d, c).transpose(0, 2, 1, 3).reshape(m - plen, c)
    return jnp.concatenate([p, s], axis=0)


def _from_strided(t, d, plen, slen):
    if d == 1:
        return t
    m, c = t.shape
    p = t[:plen].reshape(d, plen // d, c).transpose(1, 0, 2).reshape(plen, c)
    ns = (m - plen) // slen
    s = t[plen:].reshape(ns, d, slen // d, c).transpose(0, 2, 1, 3).reshape(m - plen, c)
    return jnp.concatenate([p, s], axis=0)


def _mixer_b(x, norm, w_qkv, w_o, *, plen, slen, tag):
    hw = B_HEADS * HEAD_DIM
    colscale = jnp.tile(
        jnp.concatenate([jnp.full((hw,), HEAD_DIM ** -0.5, F32), jnp.ones((2 * hw,), F32)]),
        len(B_GROUPS))
    qkv = _norm_matmul(x, norm, w_qkv.astype(BF16), colscale, name=tag + "_qkv")
    outs, lses = [], []
    for gi, (window, dil) in enumerate(B_GROUPS):
        half = (window // 2) // dil
        if dil == 1:
            grp, base = qkv, gi * 3 * B_HEADS
        else:
            grp, base = _to_strided(qkv[:, gi * 3 * hw:(gi + 1) * 3 * hw], dil, plen, slen), 0
        o, lse = _band_attn(grp, base, grp, base + B_HEADS, grp, base + 2 * B_HEADS,
                            _alibi_slopes(B_HEADS) * dil, None,
                            hq=B_HEADS, hkv=B_HEADS, half=half, plen=plen,
                            run_p=plen // dil, run_s=slen // dil, want_lse=True,
                            name="%s_attn_d%d" % (tag, dil))
        outs.append(_from_strided(o, dil, plen, slen))
        lses.append(_from_strided(lse, dil, plen, slen))
    return _merge_res(outs, lses, w_o.astype(BF16), x, name=tag + "_wo")


def _rope_tabs(plen, slen, nsample):
    pos = jnp.arange(plen, dtype=F32)
    inv = jnp.power(ROPE_THETA, -jnp.arange(0, C_ROPE, 2, dtype=F32) / C_ROPE)
    ang = pos[:, None] * inv[None, :]
    cos, sin = jnp.cos(ang), jnp.sin(ang)
    flat = lambda t: jnp.concatenate([t] + [t[:slen]] * nsample, axis=0)
    cos, sin = flat(cos), flat(sin)
    z32, z64 = jnp.zeros_like(cos), jnp.zeros((cos.shape[0], 64), F32)
    return (jnp.concatenate([cos, cos, z64], axis=1),
            jnp.concatenate([-sin, z32, z64], axis=1),
            jnp.concatenate([z32, sin, z64], axis=1))


def _mixer_c(x, norm, w_down, q_norm, kv_norm, w_uq, w_ukv, w_o, *, plen, slen, tag):
    m = x.shape[0]
    nsample = (m - plen) // slen
    tabs = _rope_tabs(plen, slen, nsample)
    w_down_p = jnp.pad(w_down, ((0, 0), (0, 128 - C_ROPE))).astype(BF16)
    cq, ckv, kr = _c_down(x, norm, w_down_p, q_norm, kv_norm, tabs)
    wq = w_uq.reshape(C_Q_RANK, C_HEADS, C_NOPE + C_ROPE)
    wq = jnp.pad(wq, ((0, 0), (0, 0), (0, 256 - C_NOPE - C_ROPE))).reshape(C_Q_RANK, C_HEADS * 256)
    q = _c_q(cq, wq.astype(BF16), tabs, scale=(C_NOPE + C_ROPE) ** -0.5 * LOG2E)
    k, vt = _c_kv(ckv, w_ukv.astype(BF16), kr, tm=FLASH_TK)
    o = _flash(q, k, vt, None, row0=0, seq=plen, nseq=1, name=tag + "_flash_p")
    o = _flash(q, k, vt, o, row0=plen, seq=slen, nseq=nsample, name=tag + "_flash_s")
    return _heads_res(o, w_o.astype(BF16), x, name=tag + "_wo")


def kernel(x_prompt, x_sample, l0_mix_norm, l0_a_w_qkv, l0_a_sink, l0_a_w_o, l0_ffn_norm, l0_ffn_w_in, l0_ffn_conv_w, l0_ffn_conv_b, l0_ffn_w_out, l1_mix_norm, l1_b_w_qkv, l1_b_w_o, l1_ffn_norm, l1_ffn_w_in, l1_ffn_conv_w, l1_ffn_conv_b, l1_ffn_w_out, l2_mix_norm, l2_c_w_down, l2_c_q_norm, l2_c_kv_norm, l2_c_w_uq, l2_c_w_ukv, l2_c_w_o, l2_ffn_norm, l2_ffn_w_in, l2_ffn_conv_w, l2_ffn_conv_b, l2_ffn_w_out, l3_mix_norm, l3_a_w_qkv, l3_a_sink, l3_a_w_o, l3_ffn_norm, l3_ffn_w_in, l3_ffn_conv_w, l3_ffn_conv_b, l3_ffn_w_out, final_norm):
    bp, sp, d = x_prompt.shape
    bs, ss, _ = x_sample.shape
    plen = bp * sp
    assert bp == 1, "prompt rows are treated as one sequence"
    x = jnp.concatenate([x_prompt.reshape(plen, d), x_sample.reshape(bs * ss, d)], axis=0)
    kw = dict(plen=plen, slen=ss)

    def ffn(x, norm, w_in, conv_w, conv_b, w_out, tag):
        return _ffn(x, norm, w_in.astype(BF16), conv_w, conv_b, w_out.astype(BF16), name=tag, **kw)

    x = _mixer_a(x, l0_mix_norm, l0_a_w_qkv, l0_a_sink, l0_a_w_o, tag="l0_a", **kw)
    x = ffn(x, l0_ffn_norm, l0_ffn_w_in, l0_ffn_conv_w, l0_ffn_conv_b, l0_ffn_w_out, "l0_ffn")
    x = _mixer_b(x, l1_mix_norm, l1_b_w_qkv, l1_b_w_o, tag="l1_b", **kw)
    x = ffn(x, l1_ffn_norm, l1_ffn_w_in, l1_ffn_conv_w, l1_ffn_conv_b, l1_ffn_w_out, "l1_ffn")
    x = _mixer_c(x, l2_mix_norm, l2_c_w_down, l2_c_q_norm, l2_c_kv_norm, l2_c_w_uq, l2_c_w_ukv,
                 l2_c_w_o, tag="l2_c", **kw)
    x = ffn(x, l2_ffn_norm, l2_ffn_w_in, l2_ffn_conv_w, l2_ffn_conv_b, l2_ffn_w_out, "l2_ffn")
    x = _mixer_a(x, l3_mix_norm, l3_a_w_qkv, l3_a_sink, l3_a_w_o, tag="l3_a", **kw)
    x = ffn(x, l3_ffn_norm, l3_ffn_w_in, l3_ffn_conv_w, l3_ffn_conv_b, l3_ffn_w_out, "l3_ffn")
    y_p = _rmsnorm(x, final_norm, row0=0, rows=plen, name="final_norm_p")
    y_s = _rmsnorm(x, final_norm, row0=plen, rows=bs * ss, name="final_norm_s")
    return (y_p.reshape(bp, sp, d), y_s.reshape(bs, ss, d))
```

```python
import functools

import jax
import jax.numpy as jnp
import numpy as np
from jax import lax
from jax.experimental import pallas as pl
from jax.experimental.pallas import tpu as pltpu

F32 = jnp.float32
BF16 = jnp.bfloat16

D_MODEL = 2048
HEAD_DIM = 128
NORM_EPS = 1e-6
A_HEADS = 16
A_KV_HEADS = 4
A_WINDOW = 128
B_HEADS = 16
B_GROUPS = ((128, 1), (512, 4), (2048, 16))
C_HEADS = 16
C_Q_RANK = 512
C_KV_RANK = 512
C_NOPE = 128
C_ROPE = 64
C_V = 128
ROPE_THETA = 10000.0
D_FF = 5632

VMEM_LIMIT_BYTES = 56 * 1024 * 1024
LOG2E = float(np.log2(np.e))
FLASH_TK = 512
MASK_BIAS = 1e30


def _params(*sem):
    return pltpu.CompilerParams(dimension_semantics=sem, vmem_limit_bytes=VMEM_LIMIT_BYTES)


def _run_bounds(r0, plen, run_p, run_s):
    in_p = r0 < plen
    start = jnp.where(in_p, (r0 // run_p) * run_p, plen + ((r0 - plen) // run_s) * run_s)
    end = start + jnp.where(in_p, run_p, run_s)
    return start, end


def _rms_rows(x, g):
    ms = jnp.mean(x * x, axis=-1, keepdims=True)
    return (x * lax.rsqrt(ms + NORM_EPS)) * g


def _rms_to_scratch(x_ref, g_ref, hs_ref, row_off, nrows, chunk=64):
    g = g_ref[...]

    def body(c, carry):
        r = pl.multiple_of(c * chunk, chunk)
        y = _rms_rows(x_ref[pl.ds(r, chunk), :], g)
        hs_ref[pl.ds(row_off + r, chunk), :] = y.astype(hs_ref.dtype)
        return carry

    lax.fori_loop(0, nrows // chunk, body, 0)


def _col_tiles(w, tn):
    k, n = w.shape
    return w.astype(BF16).reshape(k, n // tn, tn).transpose(1, 0, 2)


ROW_TILE = 1024
PERM_BLOCK = 256
PROJ_TN = 1024


def _perm_matrix(d):
    idx = np.arange(PERM_BLOCK)
    p = np.zeros((PERM_BLOCK, PERM_BLOCK), np.float32)
    p[idx, (idx % (PERM_BLOCK // d)) * d + idx // (PERM_BLOCK // d)] = 1.0
    return p


def _norm_matmul_kernel(*refs, tm, dil):
    if dil > 1:
        x_ref, g_ref, w_ref, cs_ref, p_ref, o_ref, hs_ref, hn_ref = refs
    else:
        x_ref, g_ref, w_ref, cs_ref, o_ref, hs_ref = refs

    @pl.when(pl.program_id(1) == 0)
    def _():
        if dil == 1:
            _rms_to_scratch(x_ref, g_ref, hs_ref, 0, tm)
        else:
            _rms_to_scratch(x_ref, g_ref, hn_ref, 0, tm)
            pb, rt = PERM_BLOCK // dil, tm // dil
            for b in range(tm // PERM_BLOCK):
                z = jnp.dot(p_ref[...], hn_ref[b * PERM_BLOCK:(b + 1) * PERM_BLOCK, :],
                            preferred_element_type=F32)
                for r in range(dil):
                    hs_ref[r * rt + b * pb:r * rt + (b + 1) * pb, :] = (
                        z[r * pb:(r + 1) * pb, :].astype(hs_ref.dtype))

    acc = jnp.dot(hs_ref[...], w_ref[...], preferred_element_type=F32)
    o_ref[...] = (acc * cs_ref[...]).astype(o_ref.dtype).reshape(o_ref.shape)


def _norm_matmul(x, g, w_tiles, colscale, *, dil=None, seg=None, name):
    m, k = x.shape
    nt, _, tn = w_tiles.shape
    n = nt * tn
    tm = ROW_TILE
    in_specs = [
        pl.BlockSpec((tm, k), lambda i, j: (i, 0)),
        pl.BlockSpec((1, k), lambda i, j: (0, 0)),
        pl.BlockSpec((None, k, tn), lambda i, j: (j, 0, 0)),
        pl.BlockSpec((1, tn), lambda i, j: (0, j)),
    ]
    args = [x, g.reshape(1, k), w_tiles, colscale.reshape(1, n)]
    scratch = [pltpu.VMEM((tm, k), BF16)]
    if dil is None:
        out_specs = pl.BlockSpec((tm, tn), lambda i, j: (i, j))
        out_shape = jax.ShapeDtypeStruct((m, n), BF16)
        dil = 1
    else:
        tps = seg // tm
        out_specs = pl.BlockSpec((None, dil, None, tm // dil, tn),
                                 lambda i, j: (i // tps, 0, i % tps, 0, j))
        out_shape = jax.ShapeDtypeStruct((m // seg, dil, tps, tm // dil, n), BF16)
        if dil > 1:
            in_specs.append(pl.BlockSpec((PERM_BLOCK, PERM_BLOCK), lambda i, j: (0, 0)))
            args.append(jnp.asarray(_perm_matrix(dil), BF16))
            scratch.append(pltpu.VMEM((tm, k), BF16))
    out = pl.pallas_call(
        functools.partial(_norm_matmul_kernel, tm=tm, dil=dil),
        grid=(m // tm, nt),
        in_specs=in_specs,
        out_specs=out_specs,
        out_shape=out_shape,
        scratch_shapes=scratch,
        compiler_params=_params("parallel", "arbitrary"),
        name=name,
    )(*args)
    return out.reshape(m, n)


def _matmul_res_kernel(a_ref, w_ref, x_ref, o_ref):
    o_ref[...] = x_ref[...] + jnp.dot(a_ref[...], w_ref[...], preferred_element_type=F32)


def _matmul_res(a, w, x, *, tm=1024, tn=1024, name):
    m, k = a.shape
    n = w.shape[1]
    return pl.pallas_call(
        _matmul_res_kernel,
        grid=(m // tm, n // tn),
        in_specs=[
            pl.BlockSpec((tm, k), lambda i, j: (i, 0)),
            pl.BlockSpec((k, tn), lambda i, j: (0, j)),
            pl.BlockSpec((tm, tn), lambda i, j: (i, j)),
        ],
        out_specs=pl.BlockSpec((tm, tn), lambda i, j: (i, j)),
        out_shape=jax.ShapeDtypeStruct((m, n), F32),
        compiler_params=_params("parallel", "arbitrary"),
        name=name,
    )(a, w, x)


def _merge_res_kernel(o1_ref, o2_ref, o3_ref, l1_ref, l2_ref, l3_ref, p2_ref, p3_ref, w_ref, x_ref,
                      o_ref, a_ref, z2_ref, z3_ref, *, heads, dils, tm):
    @pl.when(pl.program_id(1) == 0)
    def _():
        l1, l2, l3 = l1_ref[...], l2_ref[...], l3_ref[...]
        mx = jnp.maximum(jnp.maximum(l1, l2), l3)
        e1, e2, e3 = jnp.exp(l1 - mx), jnp.exp(l2 - mx), jnp.exp(l3 - mx)
        inv = 1.0 / (e1 + e2 + e3)
        alphas = (e1 * inv, e2 * inv, e3 * inv)
        for b in range(tm // PERM_BLOCK):
            rs = slice(b * PERM_BLOCK, (b + 1) * PERM_BLOCK)
            nat = [o1_ref[rs, :]]
            for o_ref_g, p_ref, z_ref, d in ((o2_ref, p2_ref, z2_ref, dils[1]),
                                             (o3_ref, p3_ref, z3_ref, dils[2])):
                pb = PERM_BLOCK // d
                for r in range(d):
                    z_ref[r * pb:(r + 1) * pb, :] = o_ref_g[r, b * pb:(b + 1) * pb, :]
                nat.append(jnp.dot(p_ref[...], z_ref[...], preferred_element_type=F32))
            for h in range(heads):
                cs = slice(h * HEAD_DIM, (h + 1) * HEAD_DIM)
                acc = sum(alphas[g][rs, h:h + 1] * nat[g][:, cs].astype(F32) for g in range(3))
                a_ref[rs, cs] = acc.astype(a_ref.dtype)

    o_ref[...] = x_ref[...] + jnp.dot(a_ref[...], w_ref[...], preferred_element_type=F32)


def _merge_res(os_, lses, dils, w, x, *, seg, tm=512, tn=1024, name):
    m, k = os_[0].shape
    n = w.shape[1]
    heads = lses[0].shape[1]
    assert dils[0] == 1 and ROW_TILE % tm == 0 and tm % PERM_BLOCK == 0
    tps, parts = seg // ROW_TILE, ROW_TILE // tm

    def strided(o, d):
        view = o.reshape(m // seg, d, tps, parts, tm // d, k)
        spec = pl.BlockSpec((None, d, None, None, tm // d, k),
                            lambda i, j: (i // (tps * parts), 0, (i // parts) % tps, i % parts, 0, 0))
        return view, spec

    o2, o2_spec = strided(os_[1], dils[1])
    o3, o3_spec = strided(os_[2], dils[2])
    l_spec = pl.BlockSpec((tm, heads), lambda i, j: (i, 0))
    p_spec = pl.BlockSpec((PERM_BLOCK, PERM_BLOCK), lambda i, j: (0, 0))
    return pl.pallas_call(
        functools.partial(_merge_res_kernel, heads=heads, dils=dils, tm=tm),
        grid=(m // tm, n // tn),
        in_specs=[pl.BlockSpec((tm, k), lambda i, j: (i, 0)), o2_spec, o3_spec,
                  l_spec, l_spec, l_spec, p_spec, p_spec,
                  pl.BlockSpec((k, tn), lambda i, j: (0, j)),
                  pl.BlockSpec((tm, tn), lambda i, j: (i, j))],
        out_specs=pl.BlockSpec((tm, tn), lambda i, j: (i, j)),
        out_shape=jax.ShapeDtypeStruct((m, n), F32),
        scratch_shapes=[pltpu.VMEM((tm, k), BF16), pltpu.VMEM((PERM_BLOCK, k), BF16),
                        pltpu.VMEM((PERM_BLOCK, k), BF16)],
        compiler_params=_params("parallel", "arbitrary"),
        name=name,
    )(os_[0], o2, o3, *lses, jnp.asarray(_perm_matrix(dils[1]).T, BF16),
      jnp.asarray(_perm_matrix(dils[2]).T, BF16), w, x)


def _heads_res_kernel(a_ref, w_ref, x_ref, o_ref, as_ref, *, heads):
    @pl.when(pl.program_id(1) == 0)
    def _():
        for h in range(heads):
            as_ref[:, h * HEAD_DIM:(h + 1) * HEAD_DIM] = a_ref[h]

    o_ref[...] = x_ref[...] + jnp.dot(as_ref[...], w_ref[...], preferred_element_type=F32)


def _heads_res(a, w, x, *, tm=1024, tn=1024, name):
    heads, m, dh = a.shape
    k = heads * dh
    n = w.shape[1]
    return pl.pallas_call(
        functools.partial(_heads_res_kernel, heads=heads),
        grid=(m // tm, n // tn),
        in_specs=[
            pl.BlockSpec((heads, tm, dh), lambda i, j: (0, i, 0)),
            pl.BlockSpec((k, tn), lambda i, j: (0, j)),
            pl.BlockSpec((tm, tn), lambda i, j: (i, j)),
        ],
        out_specs=pl.BlockSpec((tm, tn), lambda i, j: (i, j)),
        out_shape=jax.ShapeDtypeStruct((m, n), F32),
        scratch_shapes=[pltpu.VMEM((tm, k), BF16)],
        compiler_params=_params("parallel", "arbitrary"),
        name=name,
    )(a, w, x)


def _piece_info(row0, dil, piece, nseg_p):
    pc, pos = row0 // piece, row0 % piece
    seg, res = pc // dil, pc % dil
    in_p = seg < nseg_p
    run_pos = jnp.where(in_p, seg * piece + pos, pos)
    run_len = jnp.where(in_p, nseg_p * piece, piece)
    return pos, seg, res, in_p, run_pos, run_len


def _band_attn_kernel(*refs, hq, hkv, half, qb, dil, piece, nseg_p, has_sink, has_lse):
    it = iter(refs)
    sl_ref = next(it)
    sink_ref = next(it) if has_sink else None
    q_ref = next(it)
    kp_ref, kc_ref, kn_ref = next(it), next(it), next(it)
    vp_ref, vc_ref, vn_ref = next(it), next(it), next(it)
    o_ref = next(it)
    lse_ref = next(it) if has_lse else None

    g = hq // hkv
    nk = qb + 2 * half
    _, _, _, _, run_pos, run_len = _piece_info(pl.program_id(0) * qb, dil, piece, nseg_p)
    qi = lax.broadcasted_iota(jnp.int32, (qb, nk), 0)
    kj = lax.broadcasted_iota(jnp.int32, (qb, nk), 1)
    dist = jnp.abs(qi + half - kj)
    kpos = run_pos - half + kj
    valid = (dist <= half) & (kpos >= 0) & (kpos < run_len)
    nbias = jnp.where(valid, -dist.astype(F32), -MASK_BIAS)

    for kv in range(hkv):
        cs = slice(kv * HEAD_DIM, (kv + 1) * HEAD_DIM)
        k = jnp.concatenate([kp_ref[:, cs], kc_ref[:, cs], kn_ref[:, cs]], axis=0)
        v = jnp.concatenate([vp_ref[:, cs], vc_ref[:, cs], vn_ref[:, cs]], axis=0)
        for gi in range(g):
            h = kv * g + gi
            hc = slice(h * HEAD_DIM, (h + 1) * HEAD_DIM)
            s = lax.dot_general(q_ref[:, hc], k, (((1,), (1,)), ((), ())),
                                preferred_element_type=F32)
            s = s + sl_ref[h] * nbias
            mx = jnp.max(s, axis=-1, keepdims=True)
            if has_sink:
                mx = jnp.maximum(mx, sink_ref[h])
            p = jnp.exp(s - mx)
            den = jnp.sum(p, axis=-1, keepdims=True)
            if has_sink:
                den = den + jnp.exp(sink_ref[h] - mx)
            o = jnp.dot(p.astype(BF16), v, preferred_element_type=F32) * (1.0 / den)
            o_ref[:, hc] = o.astype(o_ref.dtype)
            if has_lse:
                lse_ref[:, h:h + 1] = mx + jnp.log(den)


def _band_attn(q_arr, q_col0, k_arr, k_col0, v_arr, v_col0, slopes, sink, *, hq, hkv, half,
               dil, seg, nseg_p, want_lse, name, qb=128):
    m = q_arr.shape[0]
    wq, wkv = hq * HEAD_DIM, hkv * HEAD_DIM
    piece = seg // dil
    qc, kc, vc = q_col0 * HEAD_DIM // wq, k_col0 * HEAD_DIM // wkv, v_col0 * HEAD_DIM // wkv

    def prev_map(c):
        def index(i):
            row0 = i * qb
            pos, sg, res, in_p, _, _ = _piece_info(row0, dil, piece, nseg_p)
            cross = ((sg - 1) * dil + res + 1) * piece - half
            row = jnp.where(pos > 0, row0 - half, jnp.where(in_p & (sg > 0), cross, 0))
            return row // half, c
        return index

    def next_map(c):
        def index(i):
            row0 = i * qb
            pos, sg, res, in_p, _, _ = _piece_info(row0, dil, piece, nseg_p)
            cross = ((sg + 1) * dil + res) * piece
            row = jnp.where(pos + qb < piece, row0 + qb,
                            jnp.where(in_p & (sg < nseg_p - 1), cross, 0))
            return row // half, c
        return index

    smem = pl.BlockSpec(memory_space=pltpu.SMEM)
    in_specs = [smem]
    args = [slopes]
    if sink is not None:
        in_specs.append(smem)
        args.append(sink)
    in_specs += [
        pl.BlockSpec((qb, wq), lambda i: (i, qc)),
        pl.BlockSpec((half, wkv), prev_map(kc)),
        pl.BlockSpec((qb, wkv), lambda i: (i, kc)),
        pl.BlockSpec((half, wkv), next_map(kc)),
        pl.BlockSpec((half, wkv), prev_map(vc)),
        pl.BlockSpec((qb, wkv), lambda i: (i, vc)),
        pl.BlockSpec((half, wkv), next_map(vc)),
    ]
    args += [q_arr, k_arr, k_arr, k_arr, v_arr, v_arr, v_arr]
    out_shape = [jax.ShapeDtypeStruct((m, wq), BF16)]
    out_specs = [pl.BlockSpec((qb, wq), lambda i: (i, 0))]
    if want_lse:
        out_shape.append(jax.ShapeDtypeStruct((m, hq), F32))
        out_specs.append(pl.BlockSpec((qb, hq), lambda i: (i, 0)))
    res = pl.pallas_call(
        functools.partial(_band_attn_kernel, hq=hq, hkv=hkv, half=half, qb=qb, dil=dil,
                          piece=piece, nseg_p=nseg_p, has_sink=sink is not None, has_lse=want_lse),
        grid=(m // qb,),
        in_specs=in_specs,
        out_specs=out_specs,
        out_shape=out_shape,
        compiler_params=_params("parallel"),
        name=name,
    )(*args)
    return res if want_lse else res[0]


def _rope128(x, cos_p, sin_a, sin_b):
    return x * cos_p + pltpu.roll(x, 96, 1) * sin_a + pltpu.roll(x, 32, 1) * sin_b


def _c_down_kernel(x_ref, g_ref, w_ref, qg_ref, kg_ref, cos_ref, sa_ref, sb_ref,
                   cq_ref, ckv_ref, kr_ref, hs_ref, *, tm):
    _rms_to_scratch(x_ref, g_ref, hs_ref, 0, tm)
    c = jnp.dot(hs_ref[...], w_ref[...], preferred_element_type=F32)
    cq_ref[...] = _rms_rows(c[:, :C_Q_RANK], qg_ref[...]).astype(cq_ref.dtype)
    ckv_ref[...] = _rms_rows(c[:, C_Q_RANK:C_Q_RANK + C_KV_RANK], kg_ref[...]).astype(ckv_ref.dtype)
    xr = c[:, C_Q_RANK + C_KV_RANK:]
    kr_ref[...] = _rope128(xr, cos_ref[...], sa_ref[...], sb_ref[...]).astype(kr_ref.dtype)


def _c_down(x, g, w_pad, qg, kg, tabs, *, tm=512):
    m, k = x.shape
    n = w_pad.shape[1]
    row = lambda w: pl.BlockSpec((tm, w), lambda i: (i, 0))
    full = lambda a, b: pl.BlockSpec((a, b), lambda i: (0, 0))
    return pl.pallas_call(
        functools.partial(_c_down_kernel, tm=tm),
        grid=(m // tm,),
        in_specs=[row(k), full(1, k), full(k, n), full(1, C_Q_RANK), full(1, C_KV_RANK),
                  row(128), row(128), row(128)],
        out_specs=[row(C_Q_RANK), row(C_KV_RANK), row(128)],
        out_shape=[jax.ShapeDtypeStruct((m, C_Q_RANK), BF16),
                   jax.ShapeDtypeStruct((m, C_KV_RANK), BF16),
                   jax.ShapeDtypeStruct((m, 128), BF16)],
        scratch_shapes=[pltpu.VMEM((tm, k), BF16)],
        compiler_params=_params("parallel"),
        name="c_down",
    )(x, g.reshape(1, k), w_pad, qg.reshape(1, -1), kg.reshape(1, -1), *tabs)


def _c_q_kernel(cq_ref, w_ref, cos_ref, sa_ref, sb_ref, q_ref, *, heads, scale):
    cq = cq_ref[...]
    cos_p, sin_a, sin_b = cos_ref[...], sa_ref[...], sb_ref[...]
    for h in range(heads):
        r = jnp.dot(cq, w_ref[:, h * 256:(h + 1) * 256], preferred_element_type=F32) * scale
        q_ref[h, :, :128] = r[:, :128].astype(q_ref.dtype)
        q_ref[h, :, 128:] = _rope128(r[:, 128:], cos_p, sin_a, sin_b).astype(q_ref.dtype)


def _c_q(cq, w, tabs, *, scale, tm=512):
    m, k = cq.shape
    heads = w.shape[1] // 256
    row = lambda w_: pl.BlockSpec((tm, w_), lambda i: (i, 0))
    return pl.pallas_call(
        functools.partial(_c_q_kernel, heads=heads, scale=scale),
        grid=(m // tm,),
        in_specs=[row(k), pl.BlockSpec(w.shape, lambda i: (0, 0)), row(128), row(128), row(128)],
        out_specs=pl.BlockSpec((heads, tm, 256), lambda i: (0, i, 0)),
        out_shape=jax.ShapeDtypeStruct((heads, m, 256), BF16),
        compiler_params=_params("parallel"),
        name="c_q_up",
    )(cq, w, *tabs)


def _c_kv_kernel(ckv_ref, w_ref, kr_ref, k_ref, vt_ref, *, heads):
    ckv = ckv_ref[...]
    kr = kr_ref[...]
    for h in range(heads):
        r = jnp.dot(ckv, w_ref[:, h * 256:(h + 1) * 256], preferred_element_type=F32)
        k_ref[h, :, :128] = r[:, :128].astype(k_ref.dtype)
        k_ref[h, :, 128:] = kr
        vt_ref[h, 0] = r[:, 128:].T.astype(vt_ref.dtype)


def _c_kv(ckv, w, kr, *, tm):
    m, k = ckv.shape
    heads = w.shape[1] // 256
    row = lambda w_: pl.BlockSpec((tm, w_), lambda i: (i, 0))
    return pl.pallas_call(
        functools.partial(_c_kv_kernel, heads=heads),
        grid=(m // tm,),
        in_specs=[row(k), pl.BlockSpec(w.shape, lambda i: (0, 0)), row(128)],
        out_specs=[pl.BlockSpec((heads, tm, 256), lambda i: (0, i, 0)),
                   pl.BlockSpec((heads, 1, C_V, tm), lambda i: (0, i, 0, 0))],
        out_shape=[jax.ShapeDtypeStruct((heads, m, 256), BF16),
                   jax.ShapeDtypeStruct((heads, m // tm, C_V, tm), BF16)],
        compiler_params=_params("parallel"),
        name="c_kv_up",
    )(ckv, w, kr)


def _flash_kernel(*refs, tk, nkt, nchain, aliased):
    q_ref, k_ref, vt_ref = refs[0], refs[1], refs[2]
    o_ref, sa_ref, sb_ref = refs[4:7] if aliased else refs[3:6]
    assert nkt % 2 == 0
    tq = q_ref.shape[0]
    tc = tq // nchain
    dv = vt_ref.shape[1]
    qs = [q_ref[c * tc:(c + 1) * tc, :] for c in range(nchain)]

    def scores(t, dst_ref):
        k = k_ref[pl.ds(pl.multiple_of(t * tk, tk), tk), :]
        for c in range(nchain):
            dst_ref[c] = lax.dot_general(k, qs[c], (((1,), (1,)), ((), ())),
                                         preferred_element_type=F32)

    def update(t, src_ref, states):
        out = []
        for c in range(nchain):
            m_prev, l_prev, acc = states[c]
            m_new = jnp.maximum(m_prev, jnp.max(src_ref[c], axis=0, keepdims=True))
            a = jnp.exp2(m_prev - m_new)
            pt = jnp.exp2(src_ref[c] - m_new)
            l_new = a * l_prev + jnp.sum(pt, axis=0, keepdims=True)
            acc = a * acc + jnp.dot(vt_ref[t], pt.astype(BF16), preferred_element_type=F32)
            out.append((m_new, l_new, acc))
        return tuple(out)

    def body(u, states):
        t = 2 * u
        scores(t + 1, sb_ref)
        states = update(t, sa_ref, states)
        scores(t + 2, sa_ref)
        return update(t + 1, sb_ref, states)

    states = tuple((jnp.full((1, tc), -jnp.inf, F32), jnp.zeros((1, tc), F32),
                    jnp.zeros((dv, tc), F32)) for _ in range(nchain))
    scores(0, sa_ref)
    states = lax.fori_loop(0, nkt // 2 - 1, body, states)
    scores(nkt - 1, sb_ref)
    states = update(nkt - 2, sa_ref, states)
    states = update(nkt - 1, sb_ref, states)
    for c in range(nchain):
        _, l_fin, acc = states[c]
        o_ref[c * tc:(c + 1) * tc, :] = (acc * (1.0 / l_fin)).T.astype(o_ref.dtype)


def _flash(q, k, vt, prev_out, *, row0, seq, nseq, tq=512, nchain=2, name):
    heads, m, dq = q.shape
    _, _, dv, tk = vt.shape
    qpt = seq // tq
    kpt = seq // tk
    in_specs = [
        pl.BlockSpec((None, tq, dq), lambda h, i: (h, row0 // tq + i, 0)),
        pl.BlockSpec((None, seq, dq), lambda h, i: (h, row0 // seq + i // qpt, 0)),
        pl.BlockSpec((None, kpt, dv, tk), lambda h, i: (h, row0 // seq + i // qpt, 0, 0)),
    ]
    args = [q, k, vt]
    aliases = {}
    if prev_out is not None:
        in_specs.append(pl.BlockSpec(memory_space=pl.ANY))
        args.append(prev_out)
        aliases = {3: 0}
    return pl.pallas_call(
        functools.partial(_flash_kernel, tk=tk, nkt=kpt, nchain=nchain,
                          aliased=prev_out is not None),
        grid=(heads, nseq * qpt),
        in_specs=in_specs,
        out_specs=pl.BlockSpec((None, tq, dv), lambda h, i: (h, row0 // tq + i, 0)),
        out_shape=jax.ShapeDtypeStruct((heads, m, dv), BF16),
        scratch_shapes=[pltpu.VMEM((nchain, tk, tq // nchain), F32)] * 2,
        input_output_aliases=aliases,
        compiler_params=_params("parallel", "arbitrary"),
        name=name,
    )(*args)


HALO = 16


def _ffn_kernel(x_ref, xp_ref, xn_ref, g_ref, wa_ref, wv_ref, cw_ref, cb_ref, wo_ref, o_ref,
                hs_ref, ga_ref, gb_ref, *, tm, nf, plen, slen):
    i = pl.program_id(0)
    j = pl.program_id(1)
    rows = tm + 2 * HALO

    @pl.when(j == 0)
    def _():
        r0 = i * tm
        seq_start, seq_end = _run_bounds(r0, plen, plen, slen)
        g = g_ref[...]
        _rms_to_scratch(x_ref, g_ref, hs_ref, HALO, tm)
        hp = jnp.where(r0 > seq_start, _rms_rows(xp_ref[...], g), 0.0)
        hn = jnp.where(r0 + tm < seq_end, _rms_rows(xn_ref[...], g), 0.0)
        z = jnp.zeros_like(hp)
        hs_ref[0:HALO, :] = jnp.concatenate([z, hp], axis=0).astype(hs_ref.dtype)
        hs_ref[HALO + tm:2 * HALO + tm, :] = jnp.concatenate([hn, z], axis=0).astype(hs_ref.dtype)
        o_ref[...] = x_ref[...]

    def step(dst_ref, src_ref):
        if dst_ref is not None:
            a = jnp.dot(hs_ref[...], wa_ref[...], preferred_element_type=F32)
            val = jnp.dot(hs_ref[HALO:HALO + tm, :], wv_ref[...], preferred_element_type=F32)
        if src_ref is not None:
            o_ref[...] += jnp.dot(src_ref[...], wo_ref[...], preferred_element_type=F32)
        if dst_ref is not None:
            a_prev = pltpu.roll(a, 1, 0)[HALO:HALO + tm]
            a_next = pltpu.roll(a, rows - 1, 0)[HALO:HALO + tm]
            cw = cw_ref[...]
            a = cw[0:1] * a_prev + cw[1:2] * a[HALO:HALO + tm] + cw[2:3] * a_next + cb_ref[...]
            gate = 0.5 * a * (1.0 + lax.erf(a * np.float32(np.sqrt(0.5))))
            dst_ref[...] = (gate * val).astype(dst_ref.dtype)

    inner = (j > 0) & (j < nf)
    pl.when(j == 0)(lambda: step(ga_ref, None))
    pl.when(inner & (j % 2 == 1))(lambda: step(gb_ref, ga_ref))
    pl.when(inner & (j % 2 == 0))(lambda: step(ga_ref, gb_ref))
    pl.when(j == nf)(lambda: step(None, gb_ref if nf % 2 == 0 else ga_ref))


FFN_TF = 512


def _ffn(x, g, w_in_tiles, conv_w, conv_b, w_out, *, plen, slen, tm=512, name):
    m, k = x.shape
    tf = w_in_tiles.shape[2]
    nf = D_FF // tf
    r8 = tm // 8
    nb8 = m // 8
    cur = lambda j: jnp.minimum(j, nf - 1)
    return pl.pallas_call(
        functools.partial(_ffn_kernel, tm=tm, nf=nf, plen=plen, slen=slen),
        grid=(m // tm, nf + 1),
        in_specs=[
            pl.BlockSpec((tm, k), lambda i, j: (i, 0)),
            pl.BlockSpec((8, k), lambda i, j: (jnp.maximum(i * r8 - 1, 0), 0)),
            pl.BlockSpec((8, k), lambda i, j: (jnp.minimum((i + 1) * r8, nb8 - 1), 0)),
            pl.BlockSpec((1, k), lambda i, j: (0, 0)),
            pl.BlockSpec((None, k, tf), lambda i, j: (cur(j), 0, 0)),
            pl.BlockSpec((None, k, tf), lambda i, j: (nf + cur(j), 0, 0)),
            pl.BlockSpec((3, tf), lambda i, j: (0, cur(j))),
            pl.BlockSpec((1, tf), lambda i, j: (0, cur(j))),
            pl.BlockSpec((tf, k), lambda i, j: (jnp.maximum(j - 1, 0), 0)),
        ],
        out_specs=pl.BlockSpec((tm, k), lambda i, j: (i, 0)),
        out_shape=jax.ShapeDtypeStruct((m, k), F32),
        scratch_shapes=[pltpu.VMEM((tm + 2 * HALO, k), BF16),
                        pltpu.VMEM((tm, tf), BF16), pltpu.VMEM((tm, tf), BF16)],
        compiler_params=_params("parallel", "arbitrary"),
        name=name,
    )(x, x, x, g.reshape(1, k), w_in_tiles, w_in_tiles, conv_w, conv_b.reshape(1, -1), w_out)


def _rmsnorm_kernel(x_ref, g_ref, o_ref):
    o_ref[...] = _rms_rows(x_ref[...], g_ref[...])


def _rmsnorm(x, g, *, row0, rows, tm=256, name):
    k = x.shape[1]
    return pl.pallas_call(
        _rmsnorm_kernel,
        grid=(rows // tm,),
        in_specs=[pl.BlockSpec((tm, k), lambda i: (row0 // tm + i, 0)),
                  pl.BlockSpec((1, k), lambda i: (0, 0))],
        out_specs=pl.BlockSpec((tm, k), lambda i: (i, 0)),
        out_shape=jax.ShapeDtypeStruct((rows, k), F32),
        compiler_params=_params("parallel"),
        name=name,
    )(x, g.reshape(1, k))


def _alibi_slopes(n):
    return jnp.power(2.0, -8.0 * jnp.arange(1, n + 1, dtype=F32) / n)


def _mixer_a(x, norm, w_qkv, sink, w_o, *, plen, slen, tag):
    nq, nk = A_HEADS * HEAD_DIM, A_KV_HEADS * HEAD_DIM
    colscale = jnp.concatenate([jnp.full((nq,), HEAD_DIM ** -0.5, F32), jnp.ones((2 * nk,), F32)])
    qkv = _norm_matmul(x, norm, _col_tiles(w_qkv, PROJ_TN), colscale, name=tag + "_qkv")
    o = _band_attn(qkv, 0, qkv, A_HEADS, qkv, A_HEADS + A_KV_HEADS, _alibi_slopes(A_HEADS), sink,
                   hq=A_HEADS, hkv=A_KV_HEADS, half=A_WINDOW, dil=1, seg=slen,
                   nseg_p=plen // slen, want_lse=False, name=tag + "_attn")
    return _matmul_res(o, w_o.astype(BF16), x, name=tag + "_wo")


def _lse_to_natural(lse, d, seg):
    m, h = lse.shape
    t = lse.reshape(m // seg, d, seg // ROW_TILE, ROW_TILE // d, h)
    return t.transpose(0, 2, 3, 1, 4).reshape(m, h)


def _mixer_b(x, norm, w_qkv, w_o, *, plen, slen, tag):
    hw = B_HEADS * HEAD_DIM
    colscale = jnp.concatenate([jnp.full((hw,), HEAD_DIM ** -0.5, F32), jnp.ones((2 * hw,), F32)])
    nseg_p = plen // slen
    outs, lses, dils = [], [], []
    for gi, (window, dil) in enumerate(B_GROUPS):
        half = (window // 2) // dil
        w_g = _col_tiles(w_qkv[:, gi * 3 * hw:(gi + 1) * 3 * hw], PROJ_TN)
        qkv = _norm_matmul(x, norm, w_g, colscale, dil=dil, seg=slen,
                           name="%s_qkv_d%d" % (tag, dil))
        o, lse = _band_attn(qkv, 0, qkv, B_HEADS, qkv, 2 * B_HEADS,
                            _alibi_slopes(B_HEADS) * dil, None,
                            hq=B_HEADS, hkv=B_HEADS, half=half, dil=dil, seg=slen, nseg_p=nseg_p,
                            want_lse=True, name="%s_attn_d%d" % (tag, dil))
        outs.append(o)
        lses.append(_lse_to_natural(lse, dil, slen))
        dils.append(dil)
    return _merge_res(outs, lses, tuple(dils), w_o.astype(BF16), x, seg=slen, name=tag + "_wo")


def _rope_tabs(plen, slen, nsample):
    pos = jnp.arange(plen, dtype=F32)
    inv = jnp.power(ROPE_THETA, -jnp.arange(0, C_ROPE, 2, dtype=F32) / C_ROPE)
    ang = pos[:, None] * inv[None, :]
    cos, sin = jnp.cos(ang), jnp.sin(ang)
    flat = lambda t: jnp.concatenate([t] + [t[:slen]] * nsample, axis=0)
    cos, sin = flat(cos), flat(sin)
    z32, z64 = jnp.zeros_like(cos), jnp.zeros((cos.shape[0], 64), F32)
    return (jnp.concatenate([cos, cos, z64], axis=1),
            jnp.concatenate([-sin, z32, z64], axis=1),
            jnp.concatenate([z32, sin, z64], axis=1))


def _mixer_c(x, norm, w_down, q_norm, kv_norm, w_uq, w_ukv, w_o, *, plen, slen, tag):
    m = x.shape[0]
    nsample = (m - plen) // slen
    tabs = _rope_tabs(plen, slen, nsample)
    w_down_p = jnp.pad(w_down, ((0, 0), (0, 128 - C_ROPE))).astype(BF16)
    cq, ckv, kr = _c_down(x, norm, w_down_p, q_norm, kv_norm, tabs)
    wq = w_uq.reshape(C_Q_RANK, C_HEADS, C_NOPE + C_ROPE)
    wq = jnp.pad(wq, ((0, 0), (0, 0), (0, 256 - C_NOPE - C_ROPE))).reshape(C_Q_RANK, C_HEADS * 256)
    q = _c_q(cq, wq.astype(BF16), tabs, scale=(C_NOPE + C_ROPE) ** -0.5 * LOG2E)
    k, vt = _c_kv(ckv, w_ukv.astype(BF16), kr, tm=FLASH_TK)
    o = _flash(q, k, vt, None, row0=0, seq=plen, nseq=1, name=tag + "_flash_p")
    o = _flash(q, k, vt, o, row0=plen, seq=slen, nseq=nsample, name=tag + "_flash_s")
    return _heads_res(o, w_o.astype(BF16), x, name=tag + "_wo")


def kernel(x_prompt, x_sample, l0_mix_norm, l0_a_w_qkv, l0_a_sink, l0_a_w_o, l0_ffn_norm, l0_ffn_w_in, l0_ffn_conv_w, l0_ffn_conv_b, l0_ffn_w_out, l1_mix_norm, l1_b_w_qkv, l1_b_w_o, l1_ffn_norm, l1_ffn_w_in, l1_ffn_conv_w, l1_ffn_conv_b, l1_ffn_w_out, l2_mix_norm, l2_c_w_down, l2_c_q_norm, l2_c_kv_norm, l2_c_w_uq, l2_c_w_ukv, l2_c_w_o, l2_ffn_norm, l2_ffn_w_in, l2_ffn_conv_w, l2_ffn_conv_b, l2_ffn_w_out, l3_mix_norm, l3_a_w_qkv, l3_a_sink, l3_a_w_o, l3_ffn_norm, l3_ffn_w_in, l3_ffn_conv_w, l3_ffn_conv_b, l3_ffn_w_out, final_norm):
    bp, sp, d = x_prompt.shape
    bs, ss, _ = x_sample.shape
    plen = bp * sp
    assert bp == 1, "prompt rows are treated as one sequence"
    x = jnp.concatenate([x_prompt.reshape(plen, d), x_sample.reshape(bs * ss, d)], axis=0)
    kw = dict(plen=plen, slen=ss)

    def ffn(x, norm, w_in, conv_w, conv_b, w_out, tag):
        return _ffn(x, norm, _col_tiles(w_in, FFN_TF), conv_w, conv_b, w_out.astype(BF16), name=tag,
                    **kw)

    x = _mixer_a(x, l0_mix_norm, l0_a_w_qkv, l0_a_sink, l0_a_w_o, tag="l0_a", **kw)
    x = ffn(x, l0_ffn_norm, l0_ffn_w_in, l0_ffn_conv_w, l0_ffn_conv_b, l0_ffn_w_out, "l0_ffn")
    x = _mixer_b(x, l1_mix_norm, l1_b_w_qkv, l1_b_w_o, tag="l1_b", **kw)
    x = ffn(x, l1_ffn_norm, l1_ffn_w_in, l1_ffn_conv_w, l1_ffn_conv_b, l1_ffn_w_out, "l1_ffn")
    x = _mixer_c(x, l2_mix_norm, l2_c_w_down, l2_c_q_norm, l2_c_kv_norm, l2_c_w_uq, l2_c_w_ukv,
                 l2_c_w_o, tag="l2_c", **kw)
    x = ffn(x, l2_ffn_norm, l2_ffn_w_in, l2_ffn_conv_w, l2_ffn_conv_b, l2_ffn_w_out, "l2_ffn")
    x = _mixer_a(x, l3_mix_norm, l3_a_w_qkv, l3_a_sink, l3_a_w_o, tag="l3_a", **kw)
    x = ffn(x, l3_ffn_norm, l3_ffn_w_in, l3_ffn_conv_w, l3_ffn_conv_b, l3_ffn_w_out, "l3_ffn")
    y_p = _rmsnorm(x, final_norm, row0=0, rows=plen, name="final_norm_p")
    y_s = _rmsnorm(x, final_norm, row0=plen, rows=bs * ss, name="final_norm_s")
    return (y_p.reshape(bp, sp, d), y_s.reshape(bs, ss, d))
```

```python
import functools

import jax
import jax.numpy as jnp
import numpy as np
from jax import lax
from jax.experimental import pallas as pl
from jax.experimental.pallas import tpu as pltpu

F32 = jnp.float32
BF16 = jnp.bfloat16

D_MODEL = 2048
HEAD_DIM = 128
NORM_EPS = 1e-6
A_HEADS = 16
A_KV_HEADS = 4
A_WINDOW = 128
B_HEADS = 16
B_GROUPS = ((128, 1), (512, 4), (2048, 16))
C_HEADS = 16
C_Q_RANK = 512
C_KV_RANK = 512
C_NOPE = 128
C_ROPE = 64
C_V = 128
ROPE_THETA = 10000.0
D_FF = 5632

VMEM_LIMIT_BYTES = 56 * 1024 * 1024
LOG2E = float(np.log2(np.e))
FLASH_TK = 512
MASK_BIAS = 1e30


def _params(*sem):
    return pltpu.CompilerParams(dimension_semantics=sem, vmem_limit_bytes=VMEM_LIMIT_BYTES)


def _run_bounds(r0, plen, run_p, run_s):
    in_p = r0 < plen
    start = jnp.where(in_p, (r0 // run_p) * run_p, plen + ((r0 - plen) // run_s) * run_s)
    end = start + jnp.where(in_p, run_p, run_s)
    return start, end


def _rms_rows(x, g):
    ms = jnp.mean(x * x, axis=-1, keepdims=True)
    return (x * lax.rsqrt(ms + NORM_EPS)) * g


def _rms_to_scratch(x_ref, g_ref, hs_ref, row_off, nrows, chunk=64):
    g = g_ref[...]

    def body(c, carry):
        r = pl.multiple_of(c * chunk, chunk)
        y = _rms_rows(x_ref[pl.ds(r, chunk), :], g)
        hs_ref[pl.ds(row_off + r, chunk), :] = y.astype(hs_ref.dtype)
        return carry

    lax.fori_loop(0, nrows // chunk, body, 0)


ROW_TILE = 1024
PERM_BLOCK = 256
PROJ_TN = 1024


def _perm_matrix(d):
    idx = np.arange(PERM_BLOCK)
    p = np.zeros((PERM_BLOCK, PERM_BLOCK), np.float32)
    p[idx, (idx % (PERM_BLOCK // d)) * d + idx // (PERM_BLOCK // d)] = 1.0
    return p


def _norm_matmul_kernel(*refs, tm, dil):
    if dil > 1:
        x_ref, g_ref, w_ref, cs_ref, p_ref, o_ref, hs_ref, hn_ref = refs
    else:
        x_ref, g_ref, w_ref, cs_ref, o_ref, hs_ref = refs

    @pl.when(pl.program_id(1) == 0)
    def _():
        if dil == 1:
            _rms_to_scratch(x_ref, g_ref, hs_ref, 0, tm)
        else:
            _rms_to_scratch(x_ref, g_ref, hn_ref, 0, tm)
            pb, rt = PERM_BLOCK // dil, tm // dil
            for b in range(tm // PERM_BLOCK):
                z = jnp.dot(p_ref[...], hn_ref[b * PERM_BLOCK:(b + 1) * PERM_BLOCK, :],
                            preferred_element_type=F32)
                for r in range(dil):
                    hs_ref[r * rt + b * pb:r * rt + (b + 1) * pb, :] = (
                        z[r * pb:(r + 1) * pb, :].astype(hs_ref.dtype))

    acc = jnp.dot(hs_ref[...], w_ref[...], preferred_element_type=F32)
    o_ref[...] = (acc * cs_ref[...]).astype(o_ref.dtype).reshape(o_ref.shape)


def _norm_matmul(x, g, w, colscale, *, dil=None, seg=None, name):
    m, k = x.shape
    n = w.shape[1]
    tm, tn = ROW_TILE, PROJ_TN
    in_specs = [
        pl.BlockSpec((tm, k), lambda i, j: (i, 0)),
        pl.BlockSpec((1, k), lambda i, j: (0, 0)),
        pl.BlockSpec((k, tn), lambda i, j: (0, j)),
        pl.BlockSpec((1, tn), lambda i, j: (0, j)),
    ]
    args = [x, g.reshape(1, k), w, colscale.reshape(1, n)]
    scratch = [pltpu.VMEM((tm, k), BF16)]
    if dil is None:
        out_specs = pl.BlockSpec((tm, tn), lambda i, j: (i, j))
        out_shape = jax.ShapeDtypeStruct((m, n), BF16)
        dil = 1
    else:
        tps = seg // tm
        out_specs = pl.BlockSpec((None, dil, None, tm // dil, tn),
                                 lambda i, j: (i // tps, 0, i % tps, 0, j))
        out_shape = jax.ShapeDtypeStruct((m // seg, dil, tps, tm // dil, n), BF16)
        if dil > 1:
            in_specs.append(pl.BlockSpec((PERM_BLOCK, PERM_BLOCK), lambda i, j: (0, 0)))
            args.append(jnp.asarray(_perm_matrix(dil), BF16))
            scratch.append(pltpu.VMEM((tm, k), BF16))
    out = pl.pallas_call(
        functools.partial(_norm_matmul_kernel, tm=tm, dil=dil),
        grid=(m // tm, n // tn),
        in_specs=in_specs,
        out_specs=out_specs,
        out_shape=out_shape,
        scratch_shapes=scratch,
        compiler_params=_params("parallel", "arbitrary"),
        name=name,
    )(*args)
    return out.reshape(m, n)


def _matmul_res_kernel(a_ref, w_ref, x_ref, o_ref):
    o_ref[...] = x_ref[...] + jnp.dot(a_ref[...], w_ref[...], preferred_element_type=F32)


def _matmul_res(a, w, x, *, tm=1024, tn=1024, name):
    m, k = a.shape
    n = w.shape[1]
    return pl.pallas_call(
        _matmul_res_kernel,
        grid=(m // tm, n // tn),
        in_specs=[
            pl.BlockSpec((tm, k), lambda i, j: (i, 0)),
            pl.BlockSpec((k, tn), lambda i, j: (0, j)),
            pl.BlockSpec((tm, tn), lambda i, j: (i, j)),
        ],
        out_specs=pl.BlockSpec((tm, tn), lambda i, j: (i, j)),
        out_shape=jax.ShapeDtypeStruct((m, n), F32),
        compiler_params=_params("parallel", "arbitrary"),
        name=name,
    )(a, w, x)


def _merge_res_kernel(o1_ref, o2_ref, o3_ref, l1_ref, l2_ref, l3_ref, p2_ref, p3_ref, w_ref, x_ref,
                      o_ref, a_ref, z2_ref, z3_ref, *, heads, dils, tm):
    @pl.when(pl.program_id(1) == 0)
    def _():
        l1, l2, l3 = l1_ref[...], l2_ref[...], l3_ref[...]
        mx = jnp.maximum(jnp.maximum(l1, l2), l3)
        e1, e2, e3 = jnp.exp(l1 - mx), jnp.exp(l2 - mx), jnp.exp(l3 - mx)
        inv = 1.0 / (e1 + e2 + e3)
        alphas = (e1 * inv, e2 * inv, e3 * inv)
        for b in range(tm // PERM_BLOCK):
            rs = slice(b * PERM_BLOCK, (b + 1) * PERM_BLOCK)
            nat = [o1_ref[rs, :]]
            for o_ref_g, p_ref, z_ref, d in ((o2_ref, p2_ref, z2_ref, dils[1]),
                                             (o3_ref, p3_ref, z3_ref, dils[2])):
                pb = PERM_BLOCK // d
                for r in range(d):
                    z_ref[r * pb:(r + 1) * pb, :] = o_ref_g[r, b * pb:(b + 1) * pb, :]
                nat.append(jnp.dot(p_ref[...], z_ref[...], preferred_element_type=F32))
            for h in range(heads):
                cs = slice(h * HEAD_DIM, (h + 1) * HEAD_DIM)
                acc = sum(alphas[g][rs, h:h + 1] * nat[g][:, cs].astype(F32) for g in range(3))
                a_ref[rs, cs] = acc.astype(a_ref.dtype)

    o_ref[...] = x_ref[...] + jnp.dot(a_ref[...], w_ref[...], preferred_element_type=F32)


def _merge_res(os_, lses, dils, w, x, *, seg, tm=512, tn=1024, name):
    m, k = os_[0].shape
    n = w.shape[1]
    heads = lses[0].shape[1]
    assert dils[0] == 1 and ROW_TILE % tm == 0 and tm % PERM_BLOCK == 0
    tps, parts = seg // ROW_TILE, ROW_TILE // tm

    def strided(o, d):
        view = o.reshape(m // seg, d, tps, parts, tm // d, k)
        spec = pl.BlockSpec((None, d, None, None, tm // d, k),
                            lambda i, j: (i // (tps * parts), 0, (i // parts) % tps, i % parts, 0, 0))
        return view, spec

    o2, o2_spec = strided(os_[1], dils[1])
    o3, o3_spec = strided(os_[2], dils[2])
    l_spec = pl.BlockSpec((tm, heads), lambda i, j: (i, 0))
    p_spec = pl.BlockSpec((PERM_BLOCK, PERM_BLOCK), lambda i, j: (0, 0))
    return pl.pallas_call(
        functools.partial(_merge_res_kernel, heads=heads, dils=dils, tm=tm),
        grid=(m // tm, n // tn),
        in_specs=[pl.BlockSpec((tm, k), lambda i, j: (i, 0)), o2_spec, o3_spec,
                  l_spec, l_spec, l_spec, p_spec, p_spec,
                  pl.BlockSpec((k, tn), lambda i, j: (0, j)),
                  pl.BlockSpec((tm, tn), lambda i, j: (i, j))],
        out_specs=pl.BlockSpec((tm, tn), lambda i, j: (i, j)),
        out_shape=jax.ShapeDtypeStruct((m, n), F32),
        scratch_shapes=[pltpu.VMEM((tm, k), BF16), pltpu.VMEM((PERM_BLOCK, k), BF16),
                        pltpu.VMEM((PERM_BLOCK, k), BF16)],
        compiler_params=_params("parallel", "arbitrary"),
        name=name,
    )(os_[0], o2, o3, *lses, jnp.asarray(_perm_matrix(dils[1]).T, BF16),
      jnp.asarray(_perm_matrix(dils[2]).T, BF16), w, x)


def _heads_res_kernel(a_ref, w_ref, x_ref, o_ref, as_ref, *, heads):
    @pl.when(pl.program_id(1) == 0)
    def _():
        for h in range(heads):
            as_ref[:, h * HEAD_DIM:(h + 1) * HEAD_DIM] = a_ref[h]

    o_ref[...] = x_ref[...] + jnp.dot(as_ref[...], w_ref[...], preferred_element_type=F32)


def _heads_res(a, w, x, *, tm=1024, tn=1024, name):
    heads, m, dh = a.shape
    k = heads * dh
    n = w.shape[1]
    return pl.pallas_call(
        functools.partial(_heads_res_kernel, heads=heads),
        grid=(m // tm, n // tn),
        in_specs=[
            pl.BlockSpec((heads, tm, dh), lambda i, j: (0, i, 0)),
            pl.BlockSpec((k, tn), lambda i, j: (0, j)),
            pl.BlockSpec((tm, tn), lambda i, j: (i, j)),
        ],
        out_specs=pl.BlockSpec((tm, tn), lambda i, j: (i, j)),
        out_shape=jax.ShapeDtypeStruct((m, n), F32),
        scratch_shapes=[pltpu.VMEM((tm, k), BF16)],
        compiler_params=_params("parallel", "arbitrary"),
        name=name,
    )(a, w, x)


def _piece_info(row0, dil, piece, nseg_p):
    pc, pos = row0 // piece, row0 % piece
    seg, res = pc // dil, pc % dil
    in_p = seg < nseg_p
    run_pos = jnp.where(in_p, seg * piece + pos, pos)
    run_len = jnp.where(in_p, nseg_p * piece, piece)
    return pos, seg, res, in_p, run_pos, run_len


def _band_attn_kernel(*refs, hq, hkv, half, qb, dil, piece, nseg_p, has_sink, has_lse):
    it = iter(refs)
    sl_ref = next(it)
    sink_ref = next(it) if has_sink else None
    q_ref = next(it)
    kp_ref, kc_ref, kn_ref = next(it), next(it), next(it)
    vp_ref, vc_ref, vn_ref = next(it), next(it), next(it)
    o_ref = next(it)
    lse_ref = next(it) if has_lse else None

    g = hq // hkv
    nk = qb + 2 * half
    _, _, _, _, run_pos, run_len = _piece_info(pl.program_id(0) * qb, dil, piece, nseg_p)
    kj = lax.broadcasted_iota(jnp.int32, (nk, qb), 0)
    qi = lax.broadcasted_iota(jnp.int32, (nk, qb), 1)
    dist = jnp.abs(qi + half - kj)
    kpos = run_pos - half + kj
    valid = (dist <= half) & (kpos >= 0) & (kpos < run_len)
    nbias = jnp.where(valid, -dist.astype(F32), -MASK_BIAS)

    for kv in range(hkv):
        cs = slice(kv * HEAD_DIM, (kv + 1) * HEAD_DIM)
        heads = range(kv * g, (kv + 1) * g)
        k = jnp.concatenate([kp_ref[:, cs], kc_ref[:, cs], kn_ref[:, cs]], axis=0)
        v = jnp.concatenate([vp_ref[:, cs], vc_ref[:, cs], vn_ref[:, cs]], axis=0)
        vt = v.astype(F32).T.astype(BF16)
        q = jnp.concatenate([q_ref[:, h * HEAD_DIM:(h + 1) * HEAD_DIM] for h in heads], axis=0)
        st = lax.dot_general(k, q, (((1,), (1,)), ((), ())), preferred_element_type=F32)
        pts, invs = [], []
        for gi, h in enumerate(heads):
            s = st[:, gi * qb:(gi + 1) * qb] + sl_ref[h] * nbias
            mx = jnp.max(s, axis=0, keepdims=True)
            if has_sink:
                mx = jnp.maximum(mx, sink_ref[h])
            p = jnp.exp2(s - mx)
            den = jnp.sum(p, axis=0, keepdims=True)
            if has_sink:
                den = den + jnp.exp2(sink_ref[h] - mx)
            pts.append(p.astype(BF16))
            invs.append(1.0 / den)
            if has_lse:
                lse_ref[h:h + 1, :] = mx * (1.0 / LOG2E) + jnp.log(den)
        ot = jnp.dot(vt, jnp.concatenate(pts, axis=1), preferred_element_type=F32)
        for gi, h in enumerate(heads):
            o = (ot[:, gi * qb:(gi + 1) * qb] * invs[gi]).T
            o_ref[:, h * HEAD_DIM:(h + 1) * HEAD_DIM] = o.astype(o_ref.dtype)


def _band_attn(q_arr, q_col0, k_arr, k_col0, v_arr, v_col0, slopes, sink, *, hq, hkv, half,
               dil, seg, nseg_p, want_lse, name, qb=128):
    m = q_arr.shape[0]
    wq, wkv = hq * HEAD_DIM, hkv * HEAD_DIM
    piece = seg // dil
    qc, kc, vc = q_col0 * HEAD_DIM // wq, k_col0 * HEAD_DIM // wkv, v_col0 * HEAD_DIM // wkv

    def prev_map(c):
        def index(i):
            row0 = i * qb
            pos, sg, res, in_p, _, _ = _piece_info(row0, dil, piece, nseg_p)
            cross = ((sg - 1) * dil + res + 1) * piece - half
            row = jnp.where(pos > 0, row0 - half, jnp.where(in_p & (sg > 0), cross, 0))
            return row // half, c
        return index

    def next_map(c):
        def index(i):
            row0 = i * qb
            pos, sg, res, in_p, _, _ = _piece_info(row0, dil, piece, nseg_p)
            cross = ((sg + 1) * dil + res) * piece
            row = jnp.where(pos + qb < piece, row0 + qb,
                            jnp.where(in_p & (sg < nseg_p - 1), cross, 0))
            return row // half, c
        return index

    smem = pl.BlockSpec(memory_space=pltpu.SMEM)
    in_specs = [smem]
    args = [slopes]
    if sink is not None:
        in_specs.append(smem)
        args.append(sink)
    in_specs += [
        pl.BlockSpec((qb, wq), lambda i: (i, qc)),
        pl.BlockSpec((half, wkv), prev_map(kc)),
        pl.BlockSpec((qb, wkv), lambda i: (i, kc)),
        pl.BlockSpec((half, wkv), next_map(kc)),
        pl.BlockSpec((half, wkv), prev_map(vc)),
        pl.BlockSpec((qb, wkv), lambda i: (i, vc)),
        pl.BlockSpec((half, wkv), next_map(vc)),
    ]
    args += [q_arr, k_arr, k_arr, k_arr, v_arr, v_arr, v_arr]
    out_shape = [jax.ShapeDtypeStruct((m, wq), BF16)]
    out_specs = [pl.BlockSpec((qb, wq), lambda i: (i, 0))]
    if want_lse:
        out_shape.append(jax.ShapeDtypeStruct((hq, m), F32))
        out_specs.append(pl.BlockSpec((hq, qb), lambda i: (0, i)))
    res = pl.pallas_call(
        functools.partial(_band_attn_kernel, hq=hq, hkv=hkv, half=half, qb=qb, dil=dil,
                          piece=piece, nseg_p=nseg_p, has_sink=sink is not None, has_lse=want_lse),
        grid=(m // qb,),
        in_specs=in_specs,
        out_specs=out_specs,
        out_shape=out_shape,
        compiler_params=_params("parallel"),
        name=name,
    )(*args)
    return res if want_lse else res[0]


def _rope128(x, cos_p, sin_a, sin_b):
    return x * cos_p + pltpu.roll(x, 96, 1) * sin_a + pltpu.roll(x, 32, 1) * sin_b


def _c_down_kernel(x_ref, g_ref, w_ref, qg_ref, kg_ref, cos_ref, sa_ref, sb_ref,
                   cq_ref, ckv_ref, kr_ref, hs_ref, *, tm):
    _rms_to_scratch(x_ref, g_ref, hs_ref, 0, tm)
    c = jnp.dot(hs_ref[...], w_ref[...], preferred_element_type=F32)
    cq_ref[...] = _rms_rows(c[:, :C_Q_RANK], qg_ref[...]).astype(cq_ref.dtype)
    ckv_ref[...] = _rms_rows(c[:, C_Q_RANK:C_Q_RANK + C_KV_RANK], kg_ref[...]).astype(ckv_ref.dtype)
    xr = c[:, C_Q_RANK + C_KV_RANK:]
    kr_ref[...] = _rope128(xr, cos_ref[...], sa_ref[...], sb_ref[...]).astype(kr_ref.dtype)


def _c_down(x, g, w_pad, qg, kg, tabs, *, tm=512):
    m, k = x.shape
    n = w_pad.shape[1]
    row = lambda w: pl.BlockSpec((tm, w), lambda i: (i, 0))
    full = lambda a, b: pl.BlockSpec((a, b), lambda i: (0, 0))
    return pl.pallas_call(
        functools.partial(_c_down_kernel, tm=tm),
        grid=(m // tm,),
        in_specs=[row(k), full(1, k), full(k, n), full(1, C_Q_RANK), full(1, C_KV_RANK),
                  row(128), row(128), row(128)],
        out_specs=[row(C_Q_RANK), row(C_KV_RANK), row(128)],
        out_shape=[jax.ShapeDtypeStruct((m, C_Q_RANK), BF16),
                   jax.ShapeDtypeStruct((m, C_KV_RANK), BF16),
                   jax.ShapeDtypeStruct((m, 128), BF16)],
        scratch_shapes=[pltpu.VMEM((tm, k), BF16)],
        compiler_params=_params("parallel"),
        name="c_down",
    )(x, g.reshape(1, k), w_pad, qg.reshape(1, -1), kg.reshape(1, -1), *tabs)


def _c_q_kernel(cq_ref, w_ref, cos_ref, sa_ref, sb_ref, q_ref, *, heads, scale):
    cq = cq_ref[...]
    cos_p, sin_a, sin_b = cos_ref[...], sa_ref[...], sb_ref[...]
    for h in range(heads):
        r = jnp.dot(cq, w_ref[:, h * 256:(h + 1) * 256], preferred_element_type=F32) * scale
        q_ref[h, :, :128] = r[:, :128].astype(q_ref.dtype)
        q_ref[h, :, 128:] = _rope128(r[:, 128:], cos_p, sin_a, sin_b).astype(q_ref.dtype)


def _c_q(cq, w, tabs, *, scale, tm=512):
    m, k = cq.shape
    heads = w.shape[1] // 256
    row = lambda w_: pl.BlockSpec((tm, w_), lambda i: (i, 0))
    return pl.pallas_call(
        functools.partial(_c_q_kernel, heads=heads, scale=scale),
        grid=(m // tm,),
        in_specs=[row(k), pl.BlockSpec(w.shape, lambda i: (0, 0)), row(128), row(128), row(128)],
        out_specs=pl.BlockSpec((heads, tm, 256), lambda i: (0, i, 0)),
        out_shape=jax.ShapeDtypeStruct((heads, m, 256), BF16),
        compiler_params=_params("parallel"),
        name="c_q_up",
    )(cq, w, *tabs)


def _c_kv_kernel(ckv_ref, w_ref, kr_ref, k_ref, vt_ref, *, heads):
    ckv = ckv_ref[...]
    kr = kr_ref[...]
    for h in range(heads):
        r = jnp.dot(ckv, w_ref[:, h * 256:(h + 1) * 256], preferred_element_type=F32)
        k_ref[h, :, :128] = r[:, :128].astype(k_ref.dtype)
        k_ref[h, :, 128:] = kr
        vt_ref[h, 0] = r[:, 128:].T.astype(vt_ref.dtype)


def _c_kv(ckv, w, kr, *, tm):
    m, k = ckv.shape
    heads = w.shape[1] // 256
    row = lambda w_: pl.BlockSpec((tm, w_), lambda i: (i, 0))
    return pl.pallas_call(
        functools.partial(_c_kv_kernel, heads=heads),
        grid=(m // tm,),
        in_specs=[row(k), pl.BlockSpec(w.shape, lambda i: (0, 0)), row(128)],
        out_specs=[pl.BlockSpec((heads, tm, 256), lambda i: (0, i, 0)),
                   pl.BlockSpec((heads, 1, C_V, tm), lambda i: (0, i, 0, 0))],
        out_shape=[jax.ShapeDtypeStruct((heads, m, 256), BF16),
                   jax.ShapeDtypeStruct((heads, m // tm, C_V, tm), BF16)],
        compiler_params=_params("parallel"),
        name="c_kv_up",
    )(ckv, w, kr)


def _flash_kernel(*refs, tk, nkt, nchain, aliased):
    q_ref, k_ref, vt_ref = refs[0], refs[1], refs[2]
    o_ref, sa_ref, sb_ref = refs[4:7] if aliased else refs[3:6]
    assert nkt % 2 == 0
    tq = q_ref.shape[0]
    tc = tq // nchain
    dv = vt_ref.shape[1]
    qs = [q_ref[c * tc:(c + 1) * tc, :] for c in range(nchain)]

    def scores(t, dst_ref):
        k = k_ref[pl.ds(pl.multiple_of(t * tk, tk), tk), :]
        for c in range(nchain):
            dst_ref[c] = lax.dot_general(k, qs[c], (((1,), (1,)), ((), ())),
                                         preferred_element_type=F32)

    def update(t, src_ref, states):
        out = []
        for c in range(nchain):
            m_prev, l_prev, acc = states[c]
            m_new = jnp.maximum(m_prev, jnp.max(src_ref[c], axis=0, keepdims=True))
            a = jnp.exp2(m_prev - m_new)
            pt = jnp.exp2(src_ref[c] - m_new)
            l_new = a * l_prev + jnp.sum(pt, axis=0, keepdims=True)
            acc = a * acc + jnp.dot(vt_ref[t], pt.astype(BF16), preferred_element_type=F32)
            out.append((m_new, l_new, acc))
        return tuple(out)

    def body(u, states):
        t = 2 * u
        scores(t + 1, sb_ref)
        states = update(t, sa_ref, states)
        scores(t + 2, sa_ref)
        return update(t + 1, sb_ref, states)

    states = tuple((jnp.full((1, tc), -jnp.inf, F32), jnp.zeros((1, tc), F32),
                    jnp.zeros((dv, tc), F32)) for _ in range(nchain))
    scores(0, sa_ref)
    states = lax.fori_loop(0, nkt // 2 - 1, body, states)
    scores(nkt - 1, sb_ref)
    states = update(nkt - 2, sa_ref, states)
    states = update(nkt - 1, sb_ref, states)
    for c in range(nchain):
        _, l_fin, acc = states[c]
        o_ref[c * tc:(c + 1) * tc, :] = (acc * (1.0 / l_fin)).T.astype(o_ref.dtype)


def _flash(q, k, vt, prev_out, *, row0, seq, nseq, tq=1024, nchain=4, name):
    heads, m, dq = q.shape
    _, _, dv, tk = vt.shape
    qpt = seq // tq
    kpt = seq // tk
    in_specs = [
        pl.BlockSpec((None, tq, dq), lambda h, i: (h, row0 // tq + i, 0)),
        pl.BlockSpec((None, seq, dq), lambda h, i: (h, row0 // seq + i // qpt, 0)),
        pl.BlockSpec((None, kpt, dv, tk), lambda h, i: (h, row0 // seq + i // qpt, 0, 0)),
    ]
    args = [q, k, vt]
    aliases = {}
    if prev_out is not None:
        in_specs.append(pl.BlockSpec(memory_space=pl.ANY))
        args.append(prev_out)
        aliases = {3: 0}
    return pl.pallas_call(
        functools.partial(_flash_kernel, tk=tk, nkt=kpt, nchain=nchain,
                          aliased=prev_out is not None),
        grid=(heads, nseq * qpt),
        in_specs=in_specs,
        out_specs=pl.BlockSpec((None, tq, dv), lambda h, i: (h, row0 // tq + i, 0)),
        out_shape=jax.ShapeDtypeStruct((heads, m, dv), BF16),
        scratch_shapes=[pltpu.VMEM((nchain, tk, tq // nchain), F32)] * 2,
        input_output_aliases=aliases,
        compiler_params=_params("parallel", "arbitrary"),
        name=name,
    )(*args)


HALO = 16


def _ffn_kernel(x_ref, xp_ref, xn_ref, g_ref, wa_ref, wv_ref, cw_ref, cb_ref, wo_ref, o_ref,
                hs_ref, *, tm, plen, slen):
    i = pl.program_id(0)
    rows = tm + 2 * HALO

    @pl.when(pl.program_id(1) == 0)
    def _():
        r0 = i * tm
        seq_start, seq_end = _run_bounds(r0, plen, plen, slen)
        g = g_ref[...]
        _rms_to_scratch(x_ref, g_ref, hs_ref, HALO, tm)
        hp = jnp.where(r0 > seq_start, _rms_rows(xp_ref[...], g), 0.0)
        hn = jnp.where(r0 + tm < seq_end, _rms_rows(xn_ref[...], g), 0.0)
        z = jnp.zeros_like(hp)
        hs_ref[0:HALO, :] = jnp.concatenate([z, hp], axis=0).astype(hs_ref.dtype)
        hs_ref[HALO + tm:2 * HALO + tm, :] = jnp.concatenate([hn, z], axis=0).astype(hs_ref.dtype)
        o_ref[...] = x_ref[...]

    a = jnp.dot(hs_ref[...], wa_ref[...], preferred_element_type=F32)
    val = jnp.dot(hs_ref[HALO:HALO + tm, :], wv_ref[...], preferred_element_type=F32)
    a_prev = pltpu.roll(a, 1, 0)[HALO:HALO + tm]
    a_next = pltpu.roll(a, rows - 1, 0)[HALO:HALO + tm]
    cw = cw_ref[...]
    a = cw[0:1] * a_prev + cw[1:2] * a[HALO:HALO + tm] + cw[2:3] * a_next + cb_ref[...]
    gate = 0.5 * a * (1.0 + lax.erf(a * np.float32(np.sqrt(0.5))))
    o_ref[...] += jnp.dot((gate * val).astype(BF16), wo_ref[...], preferred_element_type=F32)


FFN_TM = 512
FFN_TF = 512


def _ffn(x, g, w_in, conv_w, conv_b, w_out, *, plen, slen, tm=FFN_TM, tf=FFN_TF, name):
    m, k = x.shape
    nf = D_FF // tf
    r8 = tm // 8
    nb8 = m // 8
    return pl.pallas_call(
        functools.partial(_ffn_kernel, tm=tm, plen=plen, slen=slen),
        grid=(m // tm, nf),
        in_specs=[
            pl.BlockSpec((tm, k), lambda i, j: (i, 0)),
            pl.BlockSpec((8, k), lambda i, j: (jnp.maximum(i * r8 - 1, 0), 0)),
            pl.BlockSpec((8, k), lambda i, j: (jnp.minimum((i + 1) * r8, nb8 - 1), 0)),
            pl.BlockSpec((1, k), lambda i, j: (0, 0)),
            pl.BlockSpec((k, tf), lambda i, j: (0, j)),
            pl.BlockSpec((k, tf), lambda i, j: (0, nf + j)),
            pl.BlockSpec((3, tf), lambda i, j: (0, j)),
            pl.BlockSpec((1, tf), lambda i, j: (0, j)),
            pl.BlockSpec((tf, k), lambda i, j: (j, 0)),
        ],
        out_specs=pl.BlockSpec((tm, k), lambda i, j: (i, 0)),
        out_shape=jax.ShapeDtypeStruct((m, k), F32),
        scratch_shapes=[pltpu.VMEM((tm + 2 * HALO, k), BF16)],
        compiler_params=_params("parallel", "arbitrary"),
        name=name,
    )(x, x, x, g.reshape(1, k), w_in, w_in, conv_w, conv_b.reshape(1, -1), w_out)


def _rmsnorm_kernel(x_ref, g_ref, o_ref):
    o_ref[...] = _rms_rows(x_ref[...], g_ref[...])


def _rmsnorm(x, g, *, row0, rows, tm=256, name):
    k = x.shape[1]
    return pl.pallas_call(
        _rmsnorm_kernel,
        grid=(rows // tm,),
        in_specs=[pl.BlockSpec((tm, k), lambda i: (row0 // tm + i, 0)),
                  pl.BlockSpec((1, k), lambda i: (0, 0))],
        out_specs=pl.BlockSpec((tm, k), lambda i: (i, 0)),
        out_shape=jax.ShapeDtypeStruct((rows, k), F32),
        compiler_params=_params("parallel"),
        name=name,
    )(x, g.reshape(1, k))


def _alibi_slopes(n):
    return jnp.power(2.0, -8.0 * jnp.arange(1, n + 1, dtype=F32) / n)


def _mixer_a(x, norm, w_qkv, sink, w_o, *, plen, slen, tag):
    nq, nk = A_HEADS * HEAD_DIM, A_KV_HEADS * HEAD_DIM
    colscale = jnp.concatenate([jnp.full((nq,), HEAD_DIM ** -0.5 * LOG2E, F32),
                                jnp.ones((2 * nk,), F32)])
    qkv = _norm_matmul(x, norm, w_qkv.astype(BF16), colscale, name=tag + "_qkv")
    o = _band_attn(qkv, 0, qkv, A_HEADS, qkv, A_HEADS + A_KV_HEADS, _alibi_slopes(A_HEADS) * LOG2E,
                   sink * LOG2E, hq=A_HEADS, hkv=A_KV_HEADS, half=A_WINDOW, dil=1, seg=slen,
                   nseg_p=plen // slen, want_lse=False, name=tag + "_attn")
    return _matmul_res(o, w_o.astype(BF16), x, name=tag + "_wo")


def _lse_to_natural(lse, d, seg):
    h, m = lse.shape
    t = lse.reshape(h, m // seg, d, seg // ROW_TILE, ROW_TILE // d)
    return t.transpose(1, 3, 4, 2, 0).reshape(m, h)


def _mixer_b(x, norm, w_qkv, w_o, *, plen, slen, tag):
    hw = B_HEADS * HEAD_DIM
    colscale = jnp.concatenate([jnp.full((hw,), HEAD_DIM ** -0.5 * LOG2E, F32),
                                jnp.ones((2 * hw,), F32)])
    nseg_p = plen // slen
    outs, lses, dils = [], [], []
    for gi, (window, dil) in enumerate(B_GROUPS):
        half = (window // 2) // dil
        w_g = w_qkv[:, gi * 3 * hw:(gi + 1) * 3 * hw].astype(BF16)
        qkv = _norm_matmul(x, norm, w_g, colscale, dil=dil, seg=slen,
                           name="%s_qkv_d%d" % (tag, dil))
        o, lse = _band_attn(qkv, 0, qkv, B_HEADS, qkv, 2 * B_HEADS,
                            _alibi_slopes(B_HEADS) * (dil * LOG2E), None,
                            hq=B_HEADS, hkv=B_HEADS, half=half, dil=dil, seg=slen, nseg_p=nseg_p,
                            want_lse=True, name="%s_attn_d%d" % (tag, dil))
        outs.append(o)
        lses.append(_lse_to_natural(lse, dil, slen))
        dils.append(dil)
    return _merge_res(outs, lses, tuple(dils), w_o.astype(BF16), x, seg=slen, name=tag + "_wo")


def _rope_tabs(plen, slen, nsample):
    pos = jnp.arange(plen, dtype=F32)
    inv = jnp.power(ROPE_THETA, -jnp.arange(0, C_ROPE, 2, dtype=F32) / C_ROPE)
    ang = pos[:, None] * inv[None, :]
    cos, sin = jnp.cos(ang), jnp.sin(ang)
    flat = lambda t: jnp.concatenate([t] + [t[:slen]] * nsample, axis=0)
    cos, sin = flat(cos), flat(sin)
    z32, z64 = jnp.zeros_like(cos), jnp.zeros((cos.shape[0], 64), F32)
    return (jnp.concatenate([cos, cos, z64], axis=1),
            jnp.concatenate([-sin, z32, z64], axis=1),
            jnp.concatenate([z32, sin, z64], axis=1))


def _mixer_c(x, norm, w_down, q_norm, kv_norm, w_uq, w_ukv, w_o, *, plen, slen, tag):
    m = x.shape[0]
    nsample = (m - plen) // slen
    tabs = _rope_tabs(plen, slen, nsample)
    w_down_p = jnp.pad(w_down, ((0, 0), (0, 128 - C_ROPE))).astype(BF16)
    cq, ckv, kr = _c_down(x, norm, w_down_p, q_norm, kv_norm, tabs)
    wq = w_uq.reshape(C_Q_RANK, C_HEADS, C_NOPE + C_ROPE)
    wq = jnp.pad(wq, ((0, 0), (0, 0), (0, 256 - C_NOPE - C_ROPE))).reshape(C_Q_RANK, C_HEADS * 256)
    q = _c_q(cq, wq.astype(BF16), tabs, scale=(C_NOPE + C_ROPE) ** -0.5 * LOG2E)
    k, vt = _c_kv(ckv, w_ukv.astype(BF16), kr, tm=FLASH_TK)
    o = _flash(q, k, vt, None, row0=0, seq=plen, nseq=1, name=tag + "_flash_p")
    o = _flash(q, k, vt, o, row0=plen, seq=slen, nseq=nsample, name=tag + "_flash_s")
    return _heads_res(o, w_o.astype(BF16), x, name=tag + "_wo")


def kernel(x_prompt, x_sample, l0_mix_norm, l0_a_w_qkv, l0_a_sink, l0_a_w_o, l0_ffn_norm, l0_ffn_w_in, l0_ffn_conv_w, l0_ffn_conv_b, l0_ffn_w_out, l1_mix_norm, l1_b_w_qkv, l1_b_w_o, l1_ffn_norm, l1_ffn_w_in, l1_ffn_conv_w, l1_ffn_conv_b, l1_ffn_w_out, l2_mix_norm, l2_c_w_down, l2_c_q_norm, l2_c_kv_norm, l2_c_w_uq, l2_c_w_ukv, l2_c_w_o, l2_ffn_norm, l2_ffn_w_in, l2_ffn_conv_w, l2_ffn_conv_b, l2_ffn_w_out, l3_mix_norm, l3_a_w_qkv, l3_a_sink, l3_a_w_o, l3_ffn_norm, l3_ffn_w_in, l3_ffn_conv_w, l3_ffn_conv_b, l3_ffn_w_out, final_norm):
    bp, sp, d = x_prompt.shape
    bs, ss, _ = x_sample.shape
    plen = bp * sp
    assert bp == 1, "prompt rows are treated as one sequence"
    x = jnp.concatenate([x_prompt.reshape(plen, d), x_sample.reshape(bs * ss, d)], axis=0)
    kw = dict(plen=plen, slen=ss)

    def ffn(x, norm, w_in, conv_w, conv_b, w_out, tag):
        return _ffn(x, norm, w_in.astype(BF16), conv_w, conv_b, w_out.astype(BF16), name=tag, **kw)

    x = _mixer_a(x, l0_mix_norm, l0_a_w_qkv, l0_a_sink, l0_a_w_o, tag="l0_a", **kw)
    x = ffn(x, l0_ffn_norm, l0_ffn_w_in, l0_ffn_conv_w, l0_ffn_conv_b, l0_ffn_w_out, "l0_ffn")
    x = _mixer_b(x, l1_mix_norm, l1_b_w_qkv, l1_b_w_o, tag="l1_b", **kw)
    x = ffn(x, l1_ffn_norm, l1_ffn_w_in, l1_ffn_conv_w, l1_ffn_conv_b, l1_ffn_w_out, "l1_ffn")
    x = _mixer_c(x, l2_mix_norm, l2_c_w_down, l2_c_q_norm, l2_c_kv_norm, l2_c_w_uq, l2_c_w_ukv,
                 l2_c_w_o, tag="l2_c", **kw)
    x = ffn(x, l2_ffn_norm, l2_ffn_w_in, l2_ffn_conv_w, l2_ffn_conv_b, l2_ffn_w_out, "l2_ffn")
    x = _mixer_a(x, l3_mix_norm, l3_a_w_qkv, l3_a_sink, l3_a_w_o, tag="l3_a", **kw)
    x = ffn(x, l3_ffn_norm, l3_ffn_w_in, l3_ffn_conv_w, l3_ffn_conv_b, l3_ffn_w_out, "l3_ffn")
    y_p = _rmsnorm(x, final_norm, row0=0, rows=plen, name="final_norm_p")
    y_s = _rmsnorm(x, final_norm, row0=plen, rows=bs * ss, name="final_norm_s")
    return (y_p.reshape(bp, sp, d), y_s.reshape(bs, ss, d))
```

```python
import functools

import jax
import jax.numpy as jnp
import numpy as np
from jax import lax
from jax.experimental import pallas as pl
from jax.experimental.pallas import tpu as pltpu

F32 = jnp.float32
BF16 = jnp.bfloat16

D_MODEL = 2048
HEAD_DIM = 128
NORM_EPS = 1e-6
A_HEADS = 16
A_KV_HEADS = 4
A_WINDOW = 128
B_HEADS = 16
B_GROUPS = ((128, 1), (512, 4), (2048, 16))
C_HEADS = 16
C_Q_RANK = 512
C_KV_RANK = 512
C_NOPE = 128
C_ROPE = 64
C_V = 128
ROPE_THETA = 10000.0
D_FF = 5632

VMEM_LIMIT_BYTES = 56 * 1024 * 1024
LOG2E = float(np.log2(np.e))
FLASH_TK = 512
MASK_BIAS = 1e30


def _params(*sem):
    return pltpu.CompilerParams(dimension_semantics=sem, vmem_limit_bytes=VMEM_LIMIT_BYTES)


def _run_bounds(r0, plen, run_p, run_s):
    in_p = r0 < plen
    start = jnp.where(in_p, (r0 // run_p) * run_p, plen + ((r0 - plen) // run_s) * run_s)
    end = start + jnp.where(in_p, run_p, run_s)
    return start, end


def _rms_rows(x, g):
    ms = jnp.mean(x * x, axis=-1, keepdims=True)
    return (x * lax.rsqrt(ms + NORM_EPS)) * g


def _rms_to_scratch(x_ref, g_ref, hs_ref, row_off, nrows, chunk=256):
    g = g_ref[...]

    def body(c, carry):
        r = pl.multiple_of(c * chunk, chunk)
        y = _rms_rows(x_ref[pl.ds(r, chunk), :], g)
        hs_ref[pl.ds(row_off + r, chunk), :] = y.astype(hs_ref.dtype)
        return carry

    lax.fori_loop(0, nrows // chunk, body, 0)


ROW_TILE = 1024
PERM_BLOCK = 256
PROJ_TN = 1024


def _perm_matrix(d):
    idx = np.arange(PERM_BLOCK)
    p = np.zeros((PERM_BLOCK, PERM_BLOCK), np.float32)
    p[idx, (idx % (PERM_BLOCK // d)) * d + idx // (PERM_BLOCK // d)] = 1.0
    return p


def _norm_matmul_kernel(*refs, tm, dils, group_tiles):
    nperm = sum(d > 1 for d in dils)
    x_ref, g_ref, w_ref, cs_ref = refs[:4]
    p_refs = refs[4:4 + nperm]
    o_refs = refs[4 + nperm:4 + nperm + len(dils)]
    hn_ref = refs[4 + nperm + len(dils)]
    perm_refs = list(zip(p_refs, refs[5 + nperm + len(dils):]))
    lhs_refs = [hn_ref if d == 1 else perm_refs[sum(e > 1 for e in dils[:g])][1]
                for g, d in enumerate(dils)]
    j = pl.program_id(1)

    @pl.when(j == 0)
    def _():
        _rms_to_scratch(x_ref, g_ref, hn_ref, 0, tm)
        for d, (p_ref, hs_ref) in zip([d for d in dils if d > 1], perm_refs):
            pb, rt = PERM_BLOCK // d, tm // d
            for b in range(tm // PERM_BLOCK):
                z = jnp.dot(p_ref[...], hn_ref[b * PERM_BLOCK:(b + 1) * PERM_BLOCK, :],
                            preferred_element_type=F32)
                for r in range(d):
                    hs_ref[r * rt + b * pb:r * rt + (b + 1) * pb, :] = (
                        z[r * pb:(r + 1) * pb, :].astype(hs_ref.dtype))

    def project(lhs_ref, o_ref):
        acc = jnp.dot(lhs_ref[...], w_ref[...], preferred_element_type=F32)
        o_ref[...] = (acc * cs_ref[...]).astype(o_ref.dtype).reshape(o_ref.shape)

    if len(dils) == 1:
        project(lhs_refs[0], o_refs[0])
    else:
        for g in range(len(dils)):
            pl.when(j // group_tiles == g)(functools.partial(project, lhs_refs[g], o_refs[g]))


def _norm_matmul(x, g, w, colscale, *, dils=None, seg=None, name):
    m, k = x.shape
    n = w.shape[1]
    tm, tn = ROW_TILE, PROJ_TN
    in_specs = [
        pl.BlockSpec((tm, k), lambda i, j: (i, 0)),
        pl.BlockSpec((1, k), lambda i, j: (0, 0)),
        pl.BlockSpec((k, tn), lambda i, j: (0, j)),
        pl.BlockSpec((1, tn), lambda i, j: (0, j)),
    ]
    args = [x, g.reshape(1, k), w, colscale.reshape(1, n)]
    scratch = [pltpu.VMEM((tm, k), BF16)]
    if dils is None:
        out_specs = [pl.BlockSpec((tm, tn), lambda i, j: (i, j))]
        out_shape = [jax.ShapeDtypeStruct((m, n), BF16)]
        kernel_dils, gt, ng = (1,), n // tn, n
    else:
        tps = seg // tm
        ng = n // len(dils)
        gt = ng // tn
        out_specs, out_shape = [], []
        for gi, d in enumerate(dils):
            out_specs.append(pl.BlockSpec(
                (None, d, None, tm // d, tn),
                lambda i, j, gi=gi: (i // tps, 0, i % tps, 0, jnp.clip(j - gi * gt, 0, gt - 1))))
            out_shape.append(jax.ShapeDtypeStruct((m // seg, d, tps, tm // d, ng), BF16))
            if d > 1:
                in_specs.append(pl.BlockSpec((PERM_BLOCK, PERM_BLOCK), lambda i, j: (0, 0)))
                args.append(jnp.asarray(_perm_matrix(d), BF16))
                scratch.append(pltpu.VMEM((tm, k), BF16))
        kernel_dils = tuple(dils)
    outs = pl.pallas_call(
        functools.partial(_norm_matmul_kernel, tm=tm, dils=kernel_dils, group_tiles=gt),
        grid=(m // tm, n // tn),
        in_specs=in_specs,
        out_specs=out_specs,
        out_shape=out_shape,
        scratch_shapes=scratch,
        compiler_params=_params("parallel", "arbitrary"),
        name=name,
    )(*args)
    outs = [o.reshape(m, ng) for o in outs]
    return outs[0] if dils is None else outs


def _matmul_res_kernel(a_ref, w_ref, x_ref, o_ref):
    o_ref[...] = x_ref[...] + jnp.dot(a_ref[...], w_ref[...], preferred_element_type=F32)


def _matmul_res(a, w, x, *, tm=1024, tn=1024, name):
    m, k = a.shape
    n = w.shape[1]
    return pl.pallas_call(
        _matmul_res_kernel,
        grid=(m // tm, n // tn),
        in_specs=[
            pl.BlockSpec((tm, k), lambda i, j: (i, 0)),
            pl.BlockSpec((k, tn), lambda i, j: (0, j)),
            pl.BlockSpec((tm, tn), lambda i, j: (i, j)),
        ],
        out_specs=pl.BlockSpec((tm, tn), lambda i, j: (i, j)),
        out_shape=jax.ShapeDtypeStruct((m, n), F32),
        compiler_params=_params("parallel", "arbitrary"),
        name=name,
    )(a, w, x)


def _merge_res_kernel(o1_ref, o2_ref, o3_ref, l1_ref, l2_ref, l3_ref, p2_ref, p3_ref, w_ref, x_ref,
                      o_ref, a_ref, z2_ref, z3_ref, *, heads, dils, tm):
    @pl.when(pl.program_id(1) == 0)
    def _():
        l1, l2, l3 = l1_ref[...], l2_ref[...], l3_ref[...]
        mx = jnp.maximum(jnp.maximum(l1, l2), l3)
        e1, e2, e3 = jnp.exp(l1 - mx), jnp.exp(l2 - mx), jnp.exp(l3 - mx)
        inv = 1.0 / (e1 + e2 + e3)
        alphas = (e1 * inv, e2 * inv, e3 * inv)
        for b in range(tm // PERM_BLOCK):
            rs = slice(b * PERM_BLOCK, (b + 1) * PERM_BLOCK)
            nat = [o1_ref[rs, :]]
            for o_ref_g, p_ref, z_ref, d in ((o2_ref, p2_ref, z2_ref, dils[1]),
                                             (o3_ref, p3_ref, z3_ref, dils[2])):
                pb = PERM_BLOCK // d
                for r in range(d):
                    z_ref[r * pb:(r + 1) * pb, :] = o_ref_g[r, b * pb:(b + 1) * pb, :]
                nat.append(jnp.dot(p_ref[...], z_ref[...], preferred_element_type=F32))
            for h in range(heads):
                cs = slice(h * HEAD_DIM, (h + 1) * HEAD_DIM)
                acc = sum(alphas[g][rs, h:h + 1] * nat[g][:, cs].astype(F32) for g in range(3))
                a_ref[rs, cs] = acc.astype(a_ref.dtype)

    o_ref[...] = x_ref[...] + jnp.dot(a_ref[...], w_ref[...], preferred_element_type=F32)


def _merge_res(os_, lses, dils, w, x, *, seg, tm=512, tn=1024, name):
    m, k = os_[0].shape
    n = w.shape[1]
    heads = lses[0].shape[1]
    assert dils[0] == 1 and ROW_TILE % tm == 0 and tm % PERM_BLOCK == 0
    tps, parts = seg // ROW_TILE, ROW_TILE // tm

    def strided(o, d):
        view = o.reshape(m // seg, d, tps, parts, tm // d, k)
        spec = pl.BlockSpec((None, d, None, None, tm // d, k),
                            lambda i, j: (i // (tps * parts), 0, (i // parts) % tps, i % parts, 0, 0))
        return view, spec

    o2, o2_spec = strided(os_[1], dils[1])
    o3, o3_spec = strided(os_[2], dils[2])
    l_spec = pl.BlockSpec((tm, heads), lambda i, j: (i, 0))
    p_spec = pl.BlockSpec((PERM_BLOCK, PERM_BLOCK), lambda i, j: (0, 0))
    return pl.pallas_call(
        functools.partial(_merge_res_kernel, heads=heads, dils=dils, tm=tm),
        grid=(m // tm, n // tn),
        in_specs=[pl.BlockSpec((tm, k), lambda i, j: (i, 0)), o2_spec, o3_spec,
                  l_spec, l_spec, l_spec, p_spec, p_spec,
                  pl.BlockSpec((k, tn), lambda i, j: (0, j)),
                  pl.BlockSpec((tm, tn), lambda i, j: (i, j))],
        out_specs=pl.BlockSpec((tm, tn), lambda i, j: (i, j)),
        out_shape=jax.ShapeDtypeStruct((m, n), F32),
        scratch_shapes=[pltpu.VMEM((tm, k), BF16), pltpu.VMEM((PERM_BLOCK, k), BF16),
                        pltpu.VMEM((PERM_BLOCK, k), BF16)],
        compiler_params=_params("parallel", "arbitrary"),
        name=name,
    )(os_[0], o2, o3, *lses, jnp.asarray(_perm_matrix(dils[1]).T, BF16),
      jnp.asarray(_perm_matrix(dils[2]).T, BF16), w, x)


def _heads_res_kernel(a_ref, w_ref, x_ref, o_ref, as_ref, *, heads):
    @pl.when(pl.program_id(1) == 0)
    def _():
        for h in range(heads):
            as_ref[:, h * HEAD_DIM:(h + 1) * HEAD_DIM] = a_ref[h]

    o_ref[...] = x_ref[...] + jnp.dot(as_ref[...], w_ref[...], preferred_element_type=F32)


def _heads_res(a, w, x, *, tm=1024, tn=1024, name):
    heads, m, dh = a.shape
    k = heads * dh
    n = w.shape[1]
    return pl.pallas_call(
        functools.partial(_heads_res_kernel, heads=heads),
        grid=(m // tm, n // tn),
        in_specs=[
            pl.BlockSpec((heads, tm, dh), lambda i, j: (0, i, 0)),
            pl.BlockSpec((k, tn), lambda i, j: (0, j)),
            pl.BlockSpec((tm, tn), lambda i, j: (i, j)),
        ],
        out_specs=pl.BlockSpec((tm, tn), lambda i, j: (i, j)),
        out_shape=jax.ShapeDtypeStruct((m, n), F32),
        scratch_shapes=[pltpu.VMEM((tm, k), BF16)],
        compiler_params=_params("parallel", "arbitrary"),
        name=name,
    )(a, w, x)


def _piece_info(row0, dil, piece, nseg_p):
    pc, pos = row0 // piece, row0 % piece
    seg, res = pc // dil, pc % dil
    in_p = seg < nseg_p
    run_pos = jnp.where(in_p, seg * piece + pos, pos)
    run_len = jnp.where(in_p, nseg_p * piece, piece)
    return pos, seg, res, in_p, run_pos, run_len


def _band_attn_kernel(*refs, hq, hkv, half, qb, dil, piece, nseg_p, has_sink, has_lse):
    it = iter(refs)
    sl_ref = next(it)
    sink_ref = next(it) if has_sink else None
    q_ref = next(it)
    kp_ref, kc_ref, kn_ref = next(it), next(it), next(it)
    vp_ref, vc_ref, vn_ref = next(it), next(it), next(it)
    o_ref = next(it)
    lse_ref = next(it) if has_lse else None

    g = hq // hkv
    nk = qb + 2 * half
    _, _, _, _, run_pos, run_len = _piece_info(pl.program_id(0) * qb, dil, piece, nseg_p)
    kj = lax.broadcasted_iota(jnp.int32, (nk, qb), 0)
    qi = lax.broadcasted_iota(jnp.int32, (nk, qb), 1)
    dist = jnp.abs(qi + half - kj)
    kpos = run_pos - half + kj
    valid = (dist <= half) & (kpos >= 0) & (kpos < run_len)
    nbias = jnp.where(valid, -dist.astype(F32), -MASK_BIAS)

    for kv in range(hkv):
        cs = slice(kv * HEAD_DIM, (kv + 1) * HEAD_DIM)
        heads = range(kv * g, (kv + 1) * g)
        k = jnp.concatenate([kp_ref[:, cs], kc_ref[:, cs], kn_ref[:, cs]], axis=0)
        v = jnp.concatenate([vp_ref[:, cs], vc_ref[:, cs], vn_ref[:, cs]], axis=0)
        vt = v.astype(F32).T.astype(BF16)
        q = jnp.concatenate([q_ref[:, h * HEAD_DIM:(h + 1) * HEAD_DIM] for h in heads], axis=0)
        st = lax.dot_general(k, q, (((1,), (1,)), ((), ())), preferred_element_type=F32)
        pts, invs = [], []
        for gi, h in enumerate(heads):
            s = st[:, gi * qb:(gi + 1) * qb] + sl_ref[h] * nbias
            mx = jnp.max(s, axis=0, keepdims=True)
            if has_sink:
                mx = jnp.maximum(mx, sink_ref[h])
            p = jnp.exp2(s - mx)
            den = jnp.sum(p, axis=0, keepdims=True)
            if has_sink:
                den = den + jnp.exp2(sink_ref[h] - mx)
            pts.append(p.astype(BF16))
            invs.append(1.0 / den)
            if has_lse:
                lse_ref[h:h + 1, :] = mx * (1.0 / LOG2E) + jnp.log(den)
        ot = jnp.dot(vt, jnp.concatenate(pts, axis=1), preferred_element_type=F32)
        for gi, h in enumerate(heads):
            o = (ot[:, gi * qb:(gi + 1) * qb] * invs[gi]).T
            o_ref[:, h * HEAD_DIM:(h + 1) * HEAD_DIM] = o.astype(o_ref.dtype)


def _band_attn(q_arr, q_col0, k_arr, k_col0, v_arr, v_col0, slopes, sink, *, hq, hkv, half,
               dil, seg, nseg_p, want_lse, name, qb=128):
    m = q_arr.shape[0]
    wq, wkv = hq * HEAD_DIM, hkv * HEAD_DIM
    piece = seg // dil
    qc, kc, vc = q_col0 * HEAD_DIM // wq, k_col0 * HEAD_DIM // wkv, v_col0 * HEAD_DIM // wkv

    def prev_map(c):
        def index(i):
            row0 = i * qb
            pos, sg, res, in_p, _, _ = _piece_info(row0, dil, piece, nseg_p)
            cross = ((sg - 1) * dil + res + 1) * piece - half
            row = jnp.where(pos > 0, row0 - half, jnp.where(in_p & (sg > 0), cross, 0))
            return row // half, c
        return index

    def next_map(c):
        def index(i):
            row0 = i * qb
            pos, sg, res, in_p, _, _ = _piece_info(row0, dil, piece, nseg_p)
            cross = ((sg + 1) * dil + res) * piece
            row = jnp.where(pos + qb < piece, row0 + qb,
                            jnp.where(in_p & (sg < nseg_p - 1), cross, 0))
            return row // half, c
        return index

    smem = pl.BlockSpec(memory_space=pltpu.SMEM)
    in_specs = [smem]
    args = [slopes]
    if sink is not None:
        in_specs.append(smem)
        args.append(sink)
    in_specs += [
        pl.BlockSpec((qb, wq), lambda i: (i, qc)),
        pl.BlockSpec((half, wkv), prev_map(kc)),
        pl.BlockSpec((qb, wkv), lambda i: (i, kc)),
        pl.BlockSpec((half, wkv), next_map(kc)),
        pl.BlockSpec((half, wkv), prev_map(vc)),
        pl.BlockSpec((qb, wkv), lambda i: (i, vc)),
        pl.BlockSpec((half, wkv), next_map(vc)),
    ]
    args += [q_arr, k_arr, k_arr, k_arr, v_arr, v_arr, v_arr]
    out_shape = [jax.ShapeDtypeStruct((m, wq), BF16)]
    out_specs = [pl.BlockSpec((qb, wq), lambda i: (i, 0))]
    if want_lse:
        out_shape.append(jax.ShapeDtypeStruct((hq, m), F32))
        out_specs.append(pl.BlockSpec((hq, qb), lambda i: (0, i)))
    res = pl.pallas_call(
        functools.partial(_band_attn_kernel, hq=hq, hkv=hkv, half=half, qb=qb, dil=dil,
                          piece=piece, nseg_p=nseg_p, has_sink=sink is not None, has_lse=want_lse),
        grid=(m // qb,),
        in_specs=in_specs,
        out_specs=out_specs,
        out_shape=out_shape,
        compiler_params=_params("parallel"),
        name=name,
    )(*args)
    return res if want_lse else res[0]


def _rope128(x, cos_p, sin_a, sin_b):
    return x * cos_p + pltpu.roll(x, 96, 1) * sin_a + pltpu.roll(x, 32, 1) * sin_b


def _c_down_kernel(x_ref, g_ref, w_ref, qg_ref, kg_ref, cos_ref, sa_ref, sb_ref,
                   cq_ref, ckv_ref, kr_ref, hs_ref, *, tm):
    _rms_to_scratch(x_ref, g_ref, hs_ref, 0, tm)
    c = jnp.dot(hs_ref[...], w_ref[...], preferred_element_type=F32)
    cq_ref[...] = _rms_rows(c[:, :C_Q_RANK], qg_ref[...]).astype(cq_ref.dtype)
    ckv_ref[...] = _rms_rows(c[:, C_Q_RANK:C_Q_RANK + C_KV_RANK], kg_ref[...]).astype(ckv_ref.dtype)
    xr = c[:, C_Q_RANK + C_KV_RANK:]
    kr_ref[...] = _rope128(xr, cos_ref[...], sa_ref[...], sb_ref[...]).astype(kr_ref.dtype)


def _c_down(x, g, w_pad, qg, kg, tabs, *, tm=512):
    m, k = x.shape
    n = w_pad.shape[1]
    row = lambda w: pl.BlockSpec((tm, w), lambda i: (i, 0))
    full = lambda a, b: pl.BlockSpec((a, b), lambda i: (0, 0))
    return pl.pallas_call(
        functools.partial(_c_down_kernel, tm=tm),
        grid=(m // tm,),
        in_specs=[row(k), full(1, k), full(k, n), full(1, C_Q_RANK), full(1, C_KV_RANK),
                  row(128), row(128), row(128)],
        out_specs=[row(C_Q_RANK), row(C_KV_RANK), row(128)],
        out_shape=[jax.ShapeDtypeStruct((m, C_Q_RANK), BF16),
                   jax.ShapeDtypeStruct((m, C_KV_RANK), BF16),
                   jax.ShapeDtypeStruct((m, 128), BF16)],
        scratch_shapes=[pltpu.VMEM((tm, k), BF16)],
        compiler_params=_params("parallel"),
        name="c_down",
    )(x, g.reshape(1, k), w_pad, qg.reshape(1, -1), kg.reshape(1, -1), *tabs)


def _c_q_kernel(cq_ref, w_ref, cos_ref, sa_ref, sb_ref, q_ref, *, heads, scale):
    cq = cq_ref[...]
    cos_p, sin_a, sin_b = cos_ref[...], sa_ref[...], sb_ref[...]
    for h in range(heads):
        r = jnp.dot(cq, w_ref[:, h * 256:(h + 1) * 256], preferred_element_type=F32) * scale
        q_ref[h, :, :128] = r[:, :128].astype(q_ref.dtype)
        q_ref[h, :, 128:] = _rope128(r[:, 128:], cos_p, sin_a, sin_b).astype(q_ref.dtype)


def _c_q(cq, w, tabs, *, scale, tm=512):
    m, k = cq.shape
    heads = w.shape[1] // 256
    row = lambda w_: pl.BlockSpec((tm, w_), lambda i: (i, 0))
    return pl.pallas_call(
        functools.partial(_c_q_kernel, heads=heads, scale=scale),
        grid=(m // tm,),
        in_specs=[row(k), pl.BlockSpec(w.shape, lambda i: (0, 0)), row(128), row(128), row(128)],
        out_specs=pl.BlockSpec((heads, tm, 256), lambda i: (0, i, 0)),
        out_shape=jax.ShapeDtypeStruct((heads, m, 256), BF16),
        compiler_params=_params("parallel"),
        name="c_q_up",
    )(cq, w, *tabs)


def _c_kv_kernel(ckv_ref, w_ref, kr_ref, k_ref, vt_ref, *, heads):
    ckv = ckv_ref[...]
    kr = kr_ref[...]
    for h in range(heads):
        r = jnp.dot(ckv, w_ref[:, h * 256:(h + 1) * 256], preferred_element_type=F32)
        k_ref[h, :, :128] = r[:, :128].astype(k_ref.dtype)
        k_ref[h, :, 128:] = kr
        vt_ref[h, 0] = r[:, 128:].T.astype(vt_ref.dtype)


def _c_kv(ckv, w, kr, *, tm):
    m, k = ckv.shape
    heads = w.shape[1] // 256
    row = lambda w_: pl.BlockSpec((tm, w_), lambda i: (i, 0))
    return pl.pallas_call(
        functools.partial(_c_kv_kernel, heads=heads),
        grid=(m // tm,),
        in_specs=[row(k), pl.BlockSpec(w.shape, lambda i: (0, 0)), row(128)],
        out_specs=[pl.BlockSpec((heads, tm, 256), lambda i: (0, i, 0)),
                   pl.BlockSpec((heads, 1, C_V, tm), lambda i: (0, i, 0, 0))],
        out_shape=[jax.ShapeDtypeStruct((heads, m, 256), BF16),
                   jax.ShapeDtypeStruct((heads, m // tm, C_V, tm), BF16)],
        compiler_params=_params("parallel"),
        name="c_kv_up",
    )(ckv, w, kr)


def _flash_kernel(*refs, tk, nkt, nchain, aliased):
    q_ref, k_ref, vt_ref = refs[0], refs[1], refs[2]
    o_ref, sa_ref, sb_ref = refs[4:7] if aliased else refs[3:6]
    assert nkt % 2 == 0
    tq = q_ref.shape[0]
    tc = tq // nchain
    dv = vt_ref.shape[1]
    qs = [q_ref[c * tc:(c + 1) * tc, :] for c in range(nchain)]

    def scores(t, dst_ref):
        k = k_ref[pl.ds(pl.multiple_of(t * tk, tk), tk), :]
        tops = []
        for c in range(nchain):
            st = lax.dot_general(k, qs[c], (((1,), (1,)), ((), ())), preferred_element_type=F32)
            dst_ref[c] = st
            tops.append(jnp.max(st, axis=0, keepdims=True))
        return tuple(tops)

    def update(t, src_ref, tops, states):
        out = []
        for c in range(nchain):
            m_prev, l_prev, acc = states[c]
            m_new = jnp.maximum(m_prev, tops[c])
            a = jnp.exp2(m_prev - m_new)
            pt = jnp.exp2(src_ref[c] - m_new)
            l_new = a * l_prev + jnp.sum(pt, axis=0, keepdims=True)
            acc = a * acc + jnp.dot(vt_ref[t], pt.astype(BF16), preferred_element_type=F32)
            out.append((m_new, l_new, acc))
        return tuple(out)

    def body(u, carry):
        tops_a, states = carry
        t = 2 * u
        tops_b = scores(t + 1, sb_ref)
        states = update(t, sa_ref, tops_a, states)
        tops_a = scores(t + 2, sa_ref)
        return tops_a, update(t + 1, sb_ref, tops_b, states)

    states = tuple((jnp.full((1, tc), -jnp.inf, F32), jnp.zeros((1, tc), F32),
                    jnp.zeros((dv, tc), F32)) for _ in range(nchain))
    tops_a, states = lax.fori_loop(0, nkt // 2 - 1, body, (scores(0, sa_ref), states))
    tops_b = scores(nkt - 1, sb_ref)
    states = update(nkt - 2, sa_ref, tops_a, states)
    states = update(nkt - 1, sb_ref, tops_b, states)
    for c in range(nchain):
        _, l_fin, acc = states[c]
        o_ref[c * tc:(c + 1) * tc, :] = (acc * (1.0 / l_fin)).T.astype(o_ref.dtype)


def _flash(q, k, vt, prev_out, *, row0, seq, nseq, tq=1024, nchain=4, name):
    heads, m, dq = q.shape
    _, _, dv, tk = vt.shape
    qpt = seq // tq
    kpt = seq // tk
    in_specs = [
        pl.BlockSpec((None, tq, dq), lambda h, i: (h, row0 // tq + i, 0)),
        pl.BlockSpec((None, seq, dq), lambda h, i: (h, row0 // seq + i // qpt, 0)),
        pl.BlockSpec((None, kpt, dv, tk), lambda h, i: (h, row0 // seq + i // qpt, 0, 0)),
    ]
    args = [q, k, vt]
    aliases = {}
    if prev_out is not None:
        in_specs.append(pl.BlockSpec(memory_space=pl.ANY))
        args.append(prev_out)
        aliases = {3: 0}
    return pl.pallas_call(
        functools.partial(_flash_kernel, tk=tk, nkt=kpt, nchain=nchain,
                          aliased=prev_out is not None),
        grid=(heads, nseq * qpt),
        in_specs=in_specs,
        out_specs=pl.BlockSpec((None, tq, dv), lambda h, i: (h, row0 // tq + i, 0)),
        out_shape=jax.ShapeDtypeStruct((heads, m, dv), BF16),
        scratch_shapes=[pltpu.VMEM((nchain, tk, tq // nchain), F32)] * 2,
        input_output_aliases=aliases,
        compiler_params=_params("parallel", "arbitrary"),
        name=name,
    )(*args)


HALO = 16


def _ffn_kernel(x_ref, xp_ref, xn_ref, g_ref, wa_ref, wv_ref, cw_ref, cb_ref, wo_ref, o_ref,
                hs_ref, *, tm, plen, slen):
    i = pl.program_id(0)
    rows = tm + 2 * HALO

    @pl.when(pl.program_id(1) == 0)
    def _():
        r0 = i * tm
        seq_start, seq_end = _run_bounds(r0, plen, plen, slen)
        g = g_ref[...]
        _rms_to_scratch(x_ref, g_ref, hs_ref, HALO, tm)
        hp = jnp.where(r0 > seq_start, _rms_rows(xp_ref[...], g), 0.0)
        hn = jnp.where(r0 + tm < seq_end, _rms_rows(xn_ref[...], g), 0.0)
        z = jnp.zeros_like(hp)
        hs_ref[0:HALO, :] = jnp.concatenate([z, hp], axis=0).astype(hs_ref.dtype)
        hs_ref[HALO + tm:2 * HALO + tm, :] = jnp.concatenate([hn, z], axis=0).astype(hs_ref.dtype)
        o_ref[...] = x_ref[...]

    a = jnp.dot(hs_ref[...], wa_ref[...], preferred_element_type=F32)
    val = jnp.dot(hs_ref[HALO:HALO + tm, :], wv_ref[...], preferred_element_type=F32)
    a_prev = pltpu.roll(a, 1, 0)[HALO:HALO + tm]
    a_next = pltpu.roll(a, rows - 1, 0)[HALO:HALO + tm]
    cw = cw_ref[...]
    a = cw[0:1] * a_prev + cw[1:2] * a[HALO:HALO + tm] + cw[2:3] * a_next + cb_ref[...]
    gate = 0.5 * a * (1.0 + lax.erf(a * np.float32(np.sqrt(0.5))))
    o_ref[...] += jnp.dot((gate * val).astype(BF16), wo_ref[...], preferred_element_type=F32)


FFN_TM = 512
FFN_TF = 512


def _ffn(x, g, w_in, conv_w, conv_b, w_out, *, plen, slen, tm=FFN_TM, tf=FFN_TF, name):
    m, k = x.shape
    nf = D_FF // tf
    r8 = tm // 8
    nb8 = m // 8
    return pl.pallas_call(
        functools.partial(_ffn_kernel, tm=tm, plen=plen, slen=slen),
        grid=(m // tm, nf),
        in_specs=[
            pl.BlockSpec((tm, k), lambda i, j: (i, 0)),
            pl.BlockSpec((8, k), lambda i, j: (jnp.maximum(i * r8 - 1, 0), 0)),
            pl.BlockSpec((8, k), lambda i, j: (jnp.minimum((i + 1) * r8, nb8 - 1), 0)),
            pl.BlockSpec((1, k), lambda i, j: (0, 0)),
            pl.BlockSpec((k, tf), lambda i, j: (0, j)),
            pl.BlockSpec((k, tf), lambda i, j: (0, nf + j)),
            pl.BlockSpec((3, tf), lambda i, j: (0, j)),
            pl.BlockSpec((1, tf), lambda i, j: (0, j)),
            pl.BlockSpec((tf, k), lambda i, j: (j, 0)),
        ],
        out_specs=pl.BlockSpec((tm, k), lambda i, j: (i, 0)),
        out_shape=jax.ShapeDtypeStruct((m, k), F32),
        scratch_shapes=[pltpu.VMEM((tm + 2 * HALO, k), BF16)],
        compiler_params=_params("parallel", "arbitrary"),
        name=name,
    )(x, x, x, g.reshape(1, k), w_in, w_in, conv_w, conv_b.reshape(1, -1), w_out)


def _rmsnorm_kernel(x_ref, g_ref, o_ref):
    o_ref[...] = _rms_rows(x_ref[...], g_ref[...])


def _rmsnorm(x, g, *, row0, rows, tm=256, name):
    k = x.shape[1]
    return pl.pallas_call(
        _rmsnorm_kernel,
        grid=(rows // tm,),
        in_specs=[pl.BlockSpec((tm, k), lambda i: (row0 // tm + i, 0)),
                  pl.BlockSpec((1, k), lambda i: (0, 0))],
        out_specs=pl.BlockSpec((tm, k), lambda i: (i, 0)),
        out_shape=jax.ShapeDtypeStruct((rows, k), F32),
        compiler_params=_params("parallel"),
        name=name,
    )(x, g.reshape(1, k))


def _alibi_slopes(n):
    return jnp.power(2.0, -8.0 * jnp.arange(1, n + 1, dtype=F32) / n)


def _mixer_a(x, norm, w_qkv, sink, w_o, *, plen, slen, tag):
    nq, nk = A_HEADS * HEAD_DIM, A_KV_HEADS * HEAD_DIM
    colscale = jnp.concatenate([jnp.full((nq,), HEAD_DIM ** -0.5 * LOG2E, F32),
                                jnp.ones((2 * nk,), F32)])
    qkv = _norm_matmul(x, norm, w_qkv.astype(BF16), colscale, name=tag + "_qkv")
    o = _band_attn(qkv, 0, qkv, A_HEADS, qkv, A_HEADS + A_KV_HEADS, _alibi_slopes(A_HEADS) * LOG2E,
                   sink * LOG2E, hq=A_HEADS, hkv=A_KV_HEADS, half=A_WINDOW, dil=1, seg=slen,
                   nseg_p=plen // slen, want_lse=False, name=tag + "_attn")
    return _matmul_res(o, w_o.astype(BF16), x, name=tag + "_wo")


def _lse_to_natural(lse, d, seg):
    h, m = lse.shape
    t = lse.reshape(h, m // seg, d, seg // ROW_TILE, ROW_TILE // d)
    return t.transpose(1, 3, 4, 2, 0).reshape(m, h)


def _mixer_b(x, norm, w_qkv, w_o, *, plen, slen, tag):
    hw = B_HEADS * HEAD_DIM
    colscale = jnp.tile(jnp.concatenate([jnp.full((hw,), HEAD_DIM ** -0.5 * LOG2E, F32),
                                         jnp.ones((2 * hw,), F32)]), len(B_GROUPS))
    nseg_p = plen // slen
    dils = tuple(dil for _, dil in B_GROUPS)
    qkvs = _norm_matmul(x, norm, w_qkv.astype(BF16), colscale, dils=dils, seg=slen,
                        name=tag + "_qkv")
    outs, lses = [], []
    for qkv, (window, dil) in zip(qkvs, B_GROUPS):
        half = (window // 2) // dil
        o, lse = _band_attn(qkv, 0, qkv, B_HEADS, qkv, 2 * B_HEADS,
                            _alibi_slopes(B_HEADS) * (dil * LOG2E), None,
                            hq=B_HEADS, hkv=B_HEADS, half=half, dil=dil, seg=slen, nseg_p=nseg_p,
                            want_lse=True, name="%s_attn_d%d" % (tag, dil))
        outs.append(o)
        lses.append(_lse_to_natural(lse, dil, slen))
    return _merge_res(outs, lses, dils, w_o.astype(BF16), x, seg=slen, name=tag + "_wo")


def _rope_tabs(plen, slen, nsample):
    pos = jnp.arange(plen, dtype=F32)
    inv = jnp.power(ROPE_THETA, -jnp.arange(0, C_ROPE, 2, dtype=F32) / C_ROPE)
    ang = pos[:, None] * inv[None, :]
    cos, sin = jnp.cos(ang), jnp.sin(ang)
    flat = lambda t: jnp.concatenate([t] + [t[:slen]] * nsample, axis=0)
    cos, sin = flat(cos), flat(sin)
    z32, z64 = jnp.zeros_like(cos), jnp.zeros((cos.shape[0], 64), F32)
    return (jnp.concatenate([cos, cos, z64], axis=1),
            jnp.concatenate([-sin, z32, z64], axis=1),
            jnp.concatenate([z32, sin, z64], axis=1))


def _mixer_c(x, norm, w_down, q_norm, kv_norm, w_uq, w_ukv, w_o, *, plen, slen, tag):
    m = x.shape[0]
    nsample = (m - plen) // slen
    tabs = _rope_tabs(plen, slen, nsample)
    w_down_p = jnp.pad(w_down, ((0, 0), (0, 128 - C_ROPE))).astype(BF16)
    cq, ckv, kr = _c_down(x, norm, w_down_p, q_norm, kv_norm, tabs)
    wq = w_uq.reshape(C_Q_RANK, C_HEADS, C_NOPE + C_ROPE)
    wq = jnp.pad(wq, ((0, 0), (0, 0), (0, 256 - C_NOPE - C_ROPE))).reshape(C_Q_RANK, C_HEADS * 256)
    q = _c_q(cq, wq.astype(BF16), tabs, scale=(C_NOPE + C_ROPE) ** -0.5 * LOG2E)
    k, vt = _c_kv(ckv, w_ukv.astype(BF16), kr, tm=FLASH_TK)
    o = _flash(q, k, vt, None, row0=0, seq=plen, nseq=1, name=tag + "_flash_p")
    o = _flash(q, k, vt, o, row0=plen, seq=slen, nseq=nsample, name=tag + "_flash_s")
    return _heads_res(o, w_o.astype(BF16), x, name=tag + "_wo")


def kernel(x_prompt, x_sample, l0_mix_norm, l0_a_w_qkv, l0_a_sink, l0_a_w_o, l0_ffn_norm, l0_ffn_w_in, l0_ffn_conv_w, l0_ffn_conv_b, l0_ffn_w_out, l1_mix_norm, l1_b_w_qkv, l1_b_w_o, l1_ffn_norm, l1_ffn_w_in, l1_ffn_conv_w, l1_ffn_conv_b, l1_ffn_w_out, l2_mix_norm, l2_c_w_down, l2_c_q_norm, l2_c_kv_norm, l2_c_w_uq, l2_c_w_ukv, l2_c_w_o, l2_ffn_norm, l2_ffn_w_in, l2_ffn_conv_w, l2_ffn_conv_b, l2_ffn_w_out, l3_mix_norm, l3_a_w_qkv, l3_a_sink, l3_a_w_o, l3_ffn_norm, l3_ffn_w_in, l3_ffn_conv_w, l3_ffn_conv_b, l3_ffn_w_out, final_norm):
    bp, sp, d = x_prompt.shape
    bs, ss, _ = x_sample.shape
    plen = bp * sp
    assert bp == 1, "prompt rows are treated as one sequence"
    x = jnp.concatenate([x_prompt.reshape(plen, d), x_sample.reshape(bs * ss, d)], axis=0)
    kw = dict(plen=plen, slen=ss)

    def ffn(x, norm, w_in, conv_w, conv_b, w_out, tag):
        return _ffn(x, norm, w_in.astype(BF16), conv_w, conv_b, w_out.astype(BF16), name=tag, **kw)

    x = _mixer_a(x, l0_mix_norm, l0_a_w_qkv, l0_a_sink, l0_a_w_o, tag="l0_a", **kw)
    x = ffn(x, l0_ffn_norm, l0_ffn_w_in, l0_ffn_conv_w, l0_ffn_conv_b, l0_ffn_w_out, "l0_ffn")
    x = _mixer_b(x, l1_mix_norm, l1_b_w_qkv, l1_b_w_o, tag="l1_b", **kw)
    x = ffn(x, l1_ffn_norm, l1_ffn_w_in, l1_ffn_conv_w, l1_ffn_conv_b, l1_ffn_w_out, "l1_ffn")
    x = _mixer_c(x, l2_mix_norm, l2_c_w_down, l2_c_q_norm, l2_c_kv_norm, l2_c_w_uq, l2_c_w_ukv,
                 l2_c_w_o, tag="l2_c", **kw)
    x = ffn(x, l2_ffn_norm, l2_ffn_w_in, l2_ffn_conv_w, l2_ffn_conv_b, l2_ffn_w_out, "l2_ffn")
    x = _mixer_a(x, l3_mix_norm, l3_a_w_qkv, l3_a_sink, l3_a_w_o, tag="l3_a", **kw)
    x = ffn(x, l3_ffn_norm, l3_ffn_w_in, l3_ffn_conv_w, l3_ffn_conv_b, l3_ffn_w_out, "l3_ffn")
    y_p = _rmsnorm(x, final_norm, row0=0, rows=plen, name="final_norm_p")
    y_s = _rmsnorm(x, final_norm, row0=plen, rows=bs * ss, name="final_norm_s")
    return (y_p.reshape(bp, sp, d), y_s.reshape(bs, ss, d))
```

```python
import functools

import jax
import jax.numpy as jnp
import numpy as np
from jax import lax
from jax.experimental import pallas as pl
from jax.experimental.pallas import tpu as pltpu

F32 = jnp.float32
BF16 = jnp.bfloat16

D_MODEL = 2048
HEAD_DIM = 128
NORM_EPS = 1e-6
A_HEADS = 16
A_KV_HEADS = 4
A_WINDOW = 128
B_HEADS = 16
B_GROUPS = ((128, 1), (512, 4), (2048, 16))
C_HEADS = 16
C_Q_RANK = 512
C_KV_RANK = 512
C_NOPE = 128
C_ROPE = 64
C_V = 128
ROPE_THETA = 10000.0
D_FF = 5632

VMEM_LIMIT_BYTES = 56 * 1024 * 1024
LOG2E = float(np.log2(np.e))
FLASH_TK = 1024
MASK_BIAS = 1e30


def _params(*sem):
    return pltpu.CompilerParams(dimension_semantics=sem, vmem_limit_bytes=VMEM_LIMIT_BYTES)


def _run_bounds(r0, plen, run_p, run_s):
    in_p = r0 < plen
    start = jnp.where(in_p, (r0 // run_p) * run_p, plen + ((r0 - plen) // run_s) * run_s)
    end = start + jnp.where(in_p, run_p, run_s)
    return start, end


def _rms_rows(x, g):
    ms = jnp.mean(x * x, axis=-1, keepdims=True)
    return (x * lax.rsqrt(ms + NORM_EPS)) * g


def _rms_to_scratch(x_ref, g_ref, hs_ref, row_off, nrows, chunk=256):
    g = g_ref[...]

    def body(c, carry):
        r = pl.multiple_of(c * chunk, chunk)
        y = _rms_rows(x_ref[pl.ds(r, chunk), :], g)
        hs_ref[pl.ds(row_off + r, chunk), :] = y.astype(hs_ref.dtype)
        return carry

    lax.fori_loop(0, nrows // chunk, body, 0)


ROW_TILE = 1024
PERM_BLOCK = 256
PROJ_TN = 1024


def _perm_matrix(d):
    idx = np.arange(PERM_BLOCK)
    p = np.zeros((PERM_BLOCK, PERM_BLOCK), np.float32)
    p[idx, (idx % (PERM_BLOCK // d)) * d + idx // (PERM_BLOCK // d)] = 1.0
    return p


def _norm_matmul_kernel(*refs, tm, dils, group_tiles):
    nperm = sum(d > 1 for d in dils)
    x_ref, g_ref, w_ref, cs_ref = refs[:4]
    p_refs = refs[4:4 + nperm]
    o_refs = refs[4 + nperm:4 + nperm + len(dils)]
    hn_ref = refs[4 + nperm + len(dils)]
    perm_refs = list(zip(p_refs, refs[5 + nperm + len(dils):]))
    lhs_refs = [hn_ref if d == 1 else perm_refs[sum(e > 1 for e in dils[:g])][1]
                for g, d in enumerate(dils)]
    j = pl.program_id(1)

    @pl.when(j == 0)
    def _():
        _rms_to_scratch(x_ref, g_ref, hn_ref, 0, tm)
        for d, (p_ref, hs_ref) in zip([d for d in dils if d > 1], perm_refs):
            pb, rt = PERM_BLOCK // d, tm // d
            for b in range(tm // PERM_BLOCK):
                z = jnp.dot(p_ref[...], hn_ref[b * PERM_BLOCK:(b + 1) * PERM_BLOCK, :],
                            preferred_element_type=F32)
                for r in range(d):
                    hs_ref[r * rt + b * pb:r * rt + (b + 1) * pb, :] = (
                        z[r * pb:(r + 1) * pb, :].astype(hs_ref.dtype))

    def project(lhs_ref, o_ref):
        acc = jnp.dot(lhs_ref[...], w_ref[...], preferred_element_type=F32)
        o_ref[...] = (acc * cs_ref[...]).astype(o_ref.dtype).reshape(o_ref.shape)

    if len(dils) == 1:
        project(lhs_refs[0], o_refs[0])
    else:
        for g in range(len(dils)):
            pl.when(j // group_tiles == g)(functools.partial(project, lhs_refs[g], o_refs[g]))


def _norm_matmul(x, g, w, colscale, *, dils=None, seg=None, name):
    m, k = x.shape
    n = w.shape[1]
    tm, tn = ROW_TILE, PROJ_TN
    in_specs = [
        pl.BlockSpec((tm, k), lambda i, j: (i, 0)),
        pl.BlockSpec((1, k), lambda i, j: (0, 0)),
        pl.BlockSpec((k, tn), lambda i, j: (0, j)),
        pl.BlockSpec((1, tn), lambda i, j: (0, j)),
    ]
    args = [x, g.reshape(1, k), w, colscale.reshape(1, n)]
    scratch = [pltpu.VMEM((tm, k), BF16)]
    if dils is None:
        out_specs = [pl.BlockSpec((tm, tn), lambda i, j: (i, j))]
        out_shape = [jax.ShapeDtypeStruct((m, n), BF16)]
        kernel_dils, gt, ng = (1,), n // tn, n
    else:
        tps = seg // tm
        ng = n // len(dils)
        gt = ng // tn
        out_specs, out_shape = [], []
        for gi, d in enumerate(dils):
            out_specs.append(pl.BlockSpec(
                (None, d, None, tm // d, tn),
                lambda i, j, gi=gi: (i // tps, 0, i % tps, 0, jnp.clip(j - gi * gt, 0, gt - 1))))
            out_shape.append(jax.ShapeDtypeStruct((m // seg, d, tps, tm // d, ng), BF16))
            if d > 1:
                in_specs.append(pl.BlockSpec((PERM_BLOCK, PERM_BLOCK), lambda i, j: (0, 0)))
                args.append(jnp.asarray(_perm_matrix(d), BF16))
                scratch.append(pltpu.VMEM((tm, k), BF16))
        kernel_dils = tuple(dils)
    outs = pl.pallas_call(
        functools.partial(_norm_matmul_kernel, tm=tm, dils=kernel_dils, group_tiles=gt),
        grid=(m // tm, n // tn),
        in_specs=in_specs,
        out_specs=out_specs,
        out_shape=out_shape,
        scratch_shapes=scratch,
        compiler_params=_params("parallel", "arbitrary"),
        name=name,
    )(*args)
    outs = [o.reshape(m, ng) for o in outs]
    return outs[0] if dils is None else outs


def _matmul_res_kernel(a_ref, w_ref, x_ref, o_ref):
    o_ref[...] = x_ref[...] + jnp.dot(a_ref[...], w_ref[...], preferred_element_type=F32)


def _matmul_res(a, w, x, *, tm=1024, tn=1024, name):
    m, k = a.shape
    n = w.shape[1]
    return pl.pallas_call(
        _matmul_res_kernel,
        grid=(m // tm, n // tn),
        in_specs=[
            pl.BlockSpec((tm, k), lambda i, j: (i, 0)),
            pl.BlockSpec((k, tn), lambda i, j: (0, j)),
            pl.BlockSpec((tm, tn), lambda i, j: (i, j)),
        ],
        out_specs=pl.BlockSpec((tm, tn), lambda i, j: (i, j)),
        out_shape=jax.ShapeDtypeStruct((m, n), F32),
        compiler_params=_params("parallel", "arbitrary"),
        name=name,
    )(a, w, x)


def _merge_res_kernel(o1_ref, o2_ref, o3_ref, l1_ref, l2_ref, l3_ref, p2_ref, p3_ref, w_ref, x_ref,
                      o_ref, a_ref, z2_ref, z3_ref, *, heads, dils, tm):
    @pl.when(pl.program_id(1) == 0)
    def _():
        l1, l2, l3 = l1_ref[...], l2_ref[...], l3_ref[...]
        mx = jnp.maximum(jnp.maximum(l1, l2), l3)
        e1, e2, e3 = jnp.exp(l1 - mx), jnp.exp(l2 - mx), jnp.exp(l3 - mx)
        inv = 1.0 / (e1 + e2 + e3)
        alphas = (e1 * inv, e2 * inv, e3 * inv)
        for b in range(tm // PERM_BLOCK):
            rs = slice(b * PERM_BLOCK, (b + 1) * PERM_BLOCK)
            nat = [o1_ref[rs, :]]
            for o_ref_g, p_ref, z_ref, d in ((o2_ref, p2_ref, z2_ref, dils[1]),
                                             (o3_ref, p3_ref, z3_ref, dils[2])):
                pb = PERM_BLOCK // d
                for r in range(d):
                    z_ref[r * pb:(r + 1) * pb, :] = o_ref_g[r, b * pb:(b + 1) * pb, :]
                nat.append(jnp.dot(p_ref[...], z_ref[...], preferred_element_type=F32))
            for h in range(heads):
                cs = slice(h * HEAD_DIM, (h + 1) * HEAD_DIM)
                acc = sum(alphas[g][rs, h:h + 1] * nat[g][:, cs].astype(F32) for g in range(3))
                a_ref[rs, cs] = acc.astype(a_ref.dtype)

    o_ref[...] = x_ref[...] + jnp.dot(a_ref[...], w_ref[...], preferred_element_type=F32)


def _merge_res(os_, lses, dils, w, x, *, seg, tm=512, tn=1024, name):
    m, k = os_[0].shape
    n = w.shape[1]
    heads = lses[0].shape[1]
    assert dils[0] == 1 and ROW_TILE % tm == 0 and tm % PERM_BLOCK == 0
    tps, parts = seg // ROW_TILE, ROW_TILE // tm

    def strided(o, d):
        view = o.reshape(m // seg, d, tps, parts, tm // d, k)
        spec = pl.BlockSpec((None, d, None, None, tm // d, k),
                            lambda i, j: (i // (tps * parts), 0, (i // parts) % tps, i % parts, 0, 0))
        return view, spec

    o2, o2_spec = strided(os_[1], dils[1])
    o3, o3_spec = strided(os_[2], dils[2])
    l_spec = pl.BlockSpec((tm, heads), lambda i, j: (i, 0))
    p_spec = pl.BlockSpec((PERM_BLOCK, PERM_BLOCK), lambda i, j: (0, 0))
    return pl.pallas_call(
        functools.partial(_merge_res_kernel, heads=heads, dils=dils, tm=tm),
        grid=(m // tm, n // tn),
        in_specs=[pl.BlockSpec((tm, k), lambda i, j: (i, 0)), o2_spec, o3_spec,
                  l_spec, l_spec, l_spec, p_spec, p_spec,
                  pl.BlockSpec((k, tn), lambda i, j: (0, j)),
                  pl.BlockSpec((tm, tn), lambda i, j: (i, j))],
        out_specs=pl.BlockSpec((tm, tn), lambda i, j: (i, j)),
        out_shape=jax.ShapeDtypeStruct((m, n), F32),
        scratch_shapes=[pltpu.VMEM((tm, k), BF16), pltpu.VMEM((PERM_BLOCK, k), BF16),
                        pltpu.VMEM((PERM_BLOCK, k), BF16)],
        compiler_params=_params("parallel", "arbitrary"),
        name=name,
    )(os_[0], o2, o3, *lses, jnp.asarray(_perm_matrix(dils[1]).T, BF16),
      jnp.asarray(_perm_matrix(dils[2]).T, BF16), w, x)


def _heads_res_kernel(a_ref, w_ref, x_ref, o_ref, as_ref, *, heads):
    @pl.when(pl.program_id(1) == 0)
    def _():
        for h in range(heads):
            as_ref[:, h * HEAD_DIM:(h + 1) * HEAD_DIM] = a_ref[h]

    o_ref[...] = x_ref[...] + jnp.dot(as_ref[...], w_ref[...], preferred_element_type=F32)


def _heads_res(a, w, x, *, tm=1024, tn=1024, name):
    heads, m, dh = a.shape
    k = heads * dh
    n = w.shape[1]
    return pl.pallas_call(
        functools.partial(_heads_res_kernel, heads=heads),
        grid=(m // tm, n // tn),
        in_specs=[
            pl.BlockSpec((heads, tm, dh), lambda i, j: (0, i, 0)),
            pl.BlockSpec((k, tn), lambda i, j: (0, j)),
            pl.BlockSpec((tm, tn), lambda i, j: (i, j)),
        ],
        out_specs=pl.BlockSpec((tm, tn), lambda i, j: (i, j)),
        out_shape=jax.ShapeDtypeStruct((m, n), F32),
        scratch_shapes=[pltpu.VMEM((tm, k), BF16)],
        compiler_params=_params("parallel", "arbitrary"),
        name=name,
    )(a, w, x)


def _piece_info(row0, dil, piece, nseg_p):
    pc, pos = row0 // piece, row0 % piece
    seg, res = pc // dil, pc % dil
    in_p = seg < nseg_p
    run_pos = jnp.where(in_p, seg * piece + pos, pos)
    run_len = jnp.where(in_p, nseg_p * piece, piece)
    return pos, seg, res, in_p, run_pos, run_len


def _band_attn_kernel(*refs, hq, hkv, half, qb, dil, piece, nseg_p, has_sink, has_lse):
    it = iter(refs)
    sl_ref = next(it)
    sink_ref = next(it) if has_sink else None
    q_ref = next(it)
    kp_ref, kc_ref, kn_ref = next(it), next(it), next(it)
    vp_ref, vc_ref, vn_ref = next(it), next(it), next(it)
    o_ref = next(it)
    lse_ref = next(it) if has_lse else None

    g = hq // hkv
    nk = qb + 2 * half
    _, _, _, _, run_pos, run_len = _piece_info(pl.program_id(0) * qb, dil, piece, nseg_p)
    kj = lax.broadcasted_iota(jnp.int32, (nk, qb), 0)
    qi = lax.broadcasted_iota(jnp.int32, (nk, qb), 1)
    dist = jnp.abs(qi + half - kj)
    kpos = run_pos - half + kj
    valid = (dist <= half) & (kpos >= 0) & (kpos < run_len)
    nbias = jnp.where(valid, -dist.astype(F32), -MASK_BIAS)

    for kv in range(hkv):
        cs = slice(kv * HEAD_DIM, (kv + 1) * HEAD_DIM)
        heads = range(kv * g, (kv + 1) * g)
        k = jnp.concatenate([kp_ref[:, cs], kc_ref[:, cs], kn_ref[:, cs]], axis=0)
        v = jnp.concatenate([vp_ref[:, cs], vc_ref[:, cs], vn_ref[:, cs]], axis=0)
        vt = v.T
        q = jnp.concatenate([q_ref[:, h * HEAD_DIM:(h + 1) * HEAD_DIM] for h in heads], axis=0)
        st = lax.dot_general(k, q, (((1,), (1,)), ((), ())), preferred_element_type=F32)
        pts, invs = [], []
        for gi, h in enumerate(heads):
            s = st[:, gi * qb:(gi + 1) * qb] + sl_ref[h] * nbias
            mx = jnp.max(s, axis=0, keepdims=True)
            if has_sink:
                mx = jnp.maximum(mx, sink_ref[h])
            p = jnp.exp2(s - mx)
            den = jnp.sum(p, axis=0, keepdims=True)
            if has_sink:
                den = den + jnp.exp2(sink_ref[h] - mx)
            pts.append(p.astype(BF16))
            invs.append(1.0 / den)
            if has_lse:
                lse_ref[h:h + 1, :] = mx * (1.0 / LOG2E) + jnp.log(den)
        ot = jnp.dot(vt, jnp.concatenate(pts, axis=1), preferred_element_type=F32)
        for gi, h in enumerate(heads):
            o = (ot[:, gi * qb:(gi + 1) * qb] * invs[gi]).T
            o_ref[:, h * HEAD_DIM:(h + 1) * HEAD_DIM] = o.astype(o_ref.dtype)


def _band_attn(q_arr, q_col0, k_arr, k_col0, v_arr, v_col0, slopes, sink, *, hq, hkv, half,
               dil, seg, nseg_p, want_lse, name, qb=128):
    m = q_arr.shape[0]
    wq, wkv = hq * HEAD_DIM, hkv * HEAD_DIM
    piece = seg // dil
    qc, kc, vc = q_col0 * HEAD_DIM // wq, k_col0 * HEAD_DIM // wkv, v_col0 * HEAD_DIM // wkv

    def prev_map(c):
        def index(i):
            row0 = i * qb
            pos, sg, res, in_p, _, _ = _piece_info(row0, dil, piece, nseg_p)
            cross = ((sg - 1) * dil + res + 1) * piece - half
            row = jnp.where(pos > 0, row0 - half, jnp.where(in_p & (sg > 0), cross, 0))
            return row // half, c
        return index

    def next_map(c):
        def index(i):
            row0 = i * qb
            pos, sg, res, in_p, _, _ = _piece_info(row0, dil, piece, nseg_p)
            cross = ((sg + 1) * dil + res) * piece
            row = jnp.where(pos + qb < piece, row0 + qb,
                            jnp.where(in_p & (sg < nseg_p - 1), cross, 0))
            return row // half, c
        return index

    smem = pl.BlockSpec(memory_space=pltpu.SMEM)
    in_specs = [smem]
    args = [slopes]
    if sink is not None:
        in_specs.append(smem)
        args.append(sink)
    in_specs += [
        pl.BlockSpec((qb, wq), lambda i: (i, qc)),
        pl.BlockSpec((half, wkv), prev_map(kc)),
        pl.BlockSpec((qb, wkv), lambda i: (i, kc)),
        pl.BlockSpec((half, wkv), next_map(kc)),
        pl.BlockSpec((half, wkv), prev_map(vc)),
        pl.BlockSpec((qb, wkv), lambda i: (i, vc)),
        pl.BlockSpec((half, wkv), next_map(vc)),
    ]
    args += [q_arr, k_arr, k_arr, k_arr, v_arr, v_arr, v_arr]
    out_shape = [jax.ShapeDtypeStruct((m, wq), BF16)]
    out_specs = [pl.BlockSpec((qb, wq), lambda i: (i, 0))]
    if want_lse:
        out_shape.append(jax.ShapeDtypeStruct((hq, m), F32))
        out_specs.append(pl.BlockSpec((hq, qb), lambda i: (0, i)))
    res = pl.pallas_call(
        functools.partial(_band_attn_kernel, hq=hq, hkv=hkv, half=half, qb=qb, dil=dil,
                          piece=piece, nseg_p=nseg_p, has_sink=sink is not None, has_lse=want_lse),
        grid=(m // qb,),
        in_specs=in_specs,
        out_specs=out_specs,
        out_shape=out_shape,
        compiler_params=_params("parallel"),
        name=name,
    )(*args)
    return res if want_lse else res[0]


def _rope128(x, cos_p, sin_a, sin_b):
    return x * cos_p + pltpu.roll(x, 96, 1) * sin_a + pltpu.roll(x, 32, 1) * sin_b


def _c_down_kernel(x_ref, g_ref, w_ref, qg_ref, kg_ref, cos_ref, sa_ref, sb_ref,
                   cq_ref, ckv_ref, kr_ref, hs_ref, *, tm):
    _rms_to_scratch(x_ref, g_ref, hs_ref, 0, tm)
    c = jnp.dot(hs_ref[...], w_ref[...], preferred_element_type=F32)
    cq_ref[...] = _rms_rows(c[:, :C_Q_RANK], qg_ref[...]).astype(cq_ref.dtype)
    ckv_ref[...] = _rms_rows(c[:, C_Q_RANK:C_Q_RANK + C_KV_RANK], kg_ref[...]).astype(ckv_ref.dtype)
    xr = c[:, C_Q_RANK + C_KV_RANK:]
    kr_ref[...] = _rope128(xr, cos_ref[...], sa_ref[...], sb_ref[...]).astype(kr_ref.dtype)


def _c_down(x, g, w_pad, qg, kg, tabs, *, tm=512):
    m, k = x.shape
    n = w_pad.shape[1]
    row = lambda w: pl.BlockSpec((tm, w), lambda i: (i, 0))
    full = lambda a, b: pl.BlockSpec((a, b), lambda i: (0, 0))
    return pl.pallas_call(
        functools.partial(_c_down_kernel, tm=tm),
        grid=(m // tm,),
        in_specs=[row(k), full(1, k), full(k, n), full(1, C_Q_RANK), full(1, C_KV_RANK),
                  row(128), row(128), row(128)],
        out_specs=[row(C_Q_RANK), row(C_KV_RANK), row(128)],
        out_shape=[jax.ShapeDtypeStruct((m, C_Q_RANK), BF16),
                   jax.ShapeDtypeStruct((m, C_KV_RANK), BF16),
                   jax.ShapeDtypeStruct((m, 128), BF16)],
        scratch_shapes=[pltpu.VMEM((tm, k), BF16)],
        compiler_params=_params("parallel"),
        name="c_down",
    )(x, g.reshape(1, k), w_pad, qg.reshape(1, -1), kg.reshape(1, -1), *tabs)


def _c_q_kernel(cq_ref, w_ref, cos_ref, sa_ref, sb_ref, q_ref, *, heads, scale):
    cq = cq_ref[...]
    cos_p, sin_a, sin_b = cos_ref[...], sa_ref[...], sb_ref[...]
    for h in range(heads):
        r = jnp.dot(cq, w_ref[:, h * 256:(h + 1) * 256], preferred_element_type=F32) * scale
        q_ref[h, :, :128] = r[:, :128].astype(q_ref.dtype)
        q_ref[h, :, 128:] = _rope128(r[:, 128:], cos_p, sin_a, sin_b).astype(q_ref.dtype)


def _c_q(cq, w, tabs, *, scale, tm=512):
    m, k = cq.shape
    heads = w.shape[1] // 256
    row = lambda w_: pl.BlockSpec((tm, w_), lambda i: (i, 0))
    return pl.pallas_call(
        functools.partial(_c_q_kernel, heads=heads, scale=scale),
        grid=(m // tm,),
        in_specs=[row(k), pl.BlockSpec(w.shape, lambda i: (0, 0)), row(128), row(128), row(128)],
        out_specs=pl.BlockSpec((heads, tm, 256), lambda i: (0, i, 0)),
        out_shape=jax.ShapeDtypeStruct((heads, m, 256), BF16),
        compiler_params=_params("parallel"),
        name="c_q_up",
    )(cq, w, *tabs)


def _c_kv_kernel(ckv_ref, w_ref, kr_ref, k_ref, vt_ref, *, heads):
    ckv = ckv_ref[...]
    kr = kr_ref[...]
    for h in range(heads):
        r = jnp.dot(ckv, w_ref[:, h * 256:(h + 1) * 256], preferred_element_type=F32)
        k_ref[h, :, :128] = r[:, :128].astype(k_ref.dtype)
        k_ref[h, :, 128:] = kr
        vt_ref[h, 0] = r[:, 128:].T.astype(vt_ref.dtype)


def _c_kv(ckv, w, kr, *, tm):
    m, k = ckv.shape
    heads = w.shape[1] // 256
    row = lambda w_: pl.BlockSpec((tm, w_), lambda i: (i, 0))
    return pl.pallas_call(
        functools.partial(_c_kv_kernel, heads=heads),
        grid=(m // tm,),
        in_specs=[row(k), pl.BlockSpec(w.shape, lambda i: (0, 0)), row(128)],
        out_specs=[pl.BlockSpec((heads, tm, 256), lambda i: (0, i, 0)),
                   pl.BlockSpec((heads, 1, C_V, tm), lambda i: (0, i, 0, 0))],
        out_shape=[jax.ShapeDtypeStruct((heads, m, 256), BF16),
                   jax.ShapeDtypeStruct((heads, m // tm, C_V, tm), BF16)],
        compiler_params=_params("parallel"),
        name="c_kv_up",
    )(ckv, w, kr)


def _flash_kernel(*refs, tk, nkt, nchain, aliased):
    q_ref, k_ref, vt_ref = refs[0], refs[1], refs[2]
    o_ref, sa_ref, sb_ref = refs[4:7] if aliased else refs[3:6]
    assert nkt % 2 == 0
    tq = q_ref.shape[0]
    tc = tq // nchain
    dv = vt_ref.shape[1]
    qs = [q_ref[c * tc:(c + 1) * tc, :] for c in range(nchain)]

    def scores(t, dst_ref):
        k = k_ref[pl.ds(pl.multiple_of(t * tk, tk), tk), :]
        tops = []
        for c in range(nchain):
            st = lax.dot_general(k, qs[c], (((1,), (1,)), ((), ())), preferred_element_type=F32)
            dst_ref[c] = st
            tops.append(jnp.max(st, axis=0, keepdims=True))
        return tuple(tops)

    def update(t, src_ref, tops, states):
        out = []
        for c in range(nchain):
            m_prev, l_prev, acc = states[c]
            m_new = jnp.maximum(m_prev, tops[c])
            a = jnp.exp2(m_prev - m_new)
            pt = jnp.exp2(src_ref[c] - m_new)
            l_new = a * l_prev + jnp.sum(pt, axis=0, keepdims=True)
            acc = a * acc + jnp.dot(vt_ref[t], pt.astype(BF16), preferred_element_type=F32)
            out.append((m_new, l_new, acc))
        return tuple(out)

    def body(u, carry):
        tops_a, states = carry
        t = 2 * u
        tops_b = scores(t + 1, sb_ref)
        states = update(t, sa_ref, tops_a, states)
        tops_a = scores(t + 2, sa_ref)
        return tops_a, update(t + 1, sb_ref, tops_b, states)

    states = tuple((jnp.full((1, tc), -jnp.inf, F32), jnp.zeros((1, tc), F32),
                    jnp.zeros((dv, tc), F32)) for _ in range(nchain))
    tops_a, states = lax.fori_loop(0, nkt // 2 - 1, body, (scores(0, sa_ref), states))
    tops_b = scores(nkt - 1, sb_ref)
    states = update(nkt - 2, sa_ref, tops_a, states)
    states = update(nkt - 1, sb_ref, tops_b, states)
    for c in range(nchain):
        _, l_fin, acc = states[c]
        o_ref[c * tc:(c + 1) * tc, :] = (acc * (1.0 / l_fin)).T.astype(o_ref.dtype)


def _flash(q, k, vt, prev_out, *, row0, seq, nseq, tq=1024, nchain=4, name):
    heads, m, dq = q.shape
    _, _, dv, tk = vt.shape
    qpt = seq // tq
    kpt = seq // tk
    in_specs = [
        pl.BlockSpec((None, tq, dq), lambda h, i: (h, row0 // tq + i, 0)),
        pl.BlockSpec((None, seq, dq), lambda h, i: (h, row0 // seq + i // qpt, 0)),
        pl.BlockSpec((None, kpt, dv, tk), lambda h, i: (h, row0 // seq + i // qpt, 0, 0)),
    ]
    args = [q, k, vt]
    aliases = {}
    if prev_out is not None:
        in_specs.append(pl.BlockSpec(memory_space=pl.ANY))
        args.append(prev_out)
        aliases = {3: 0}
    return pl.pallas_call(
        functools.partial(_flash_kernel, tk=tk, nkt=kpt, nchain=nchain,
                          aliased=prev_out is not None),
        grid=(heads, nseq * qpt),
        in_specs=in_specs,
        out_specs=pl.BlockSpec((None, tq, dv), lambda h, i: (h, row0 // tq + i, 0)),
        out_shape=jax.ShapeDtypeStruct((heads, m, dv), BF16),
        scratch_shapes=[pltpu.VMEM((nchain, tk, tq // nchain), F32)] * 2,
        input_output_aliases=aliases,
        compiler_params=_params("parallel", "arbitrary"),
        name=name,
    )(*args)


HALO = 16


def _ffn_kernel(x_ref, xp_ref, xn_ref, g_ref, wa_ref, wv_ref, cw_ref, cb_ref, wo_ref, o_ref,
                hs_ref, *, tm, plen, slen):
    i = pl.program_id(0)
    rows = tm + 2 * HALO

    @pl.when(pl.program_id(1) == 0)
    def _():
        r0 = i * tm
        seq_start, seq_end = _run_bounds(r0, plen, plen, slen)
        g = g_ref[...]
        _rms_to_scratch(x_ref, g_ref, hs_ref, HALO, tm)
        hp = jnp.where(r0 > seq_start, _rms_rows(xp_ref[...], g), 0.0)
        hn = jnp.where(r0 + tm < seq_end, _rms_rows(xn_ref[...], g), 0.0)
        z = jnp.zeros_like(hp)
        hs_ref[0:HALO, :] = jnp.concatenate([z, hp], axis=0).astype(hs_ref.dtype)
        hs_ref[HALO + tm:2 * HALO + tm, :] = jnp.concatenate([hn, z], axis=0).astype(hs_ref.dtype)
        o_ref[...] = x_ref[...]

    a = jnp.dot(hs_ref[...], wa_ref[...], preferred_element_type=F32)
    val = jnp.dot(hs_ref[HALO:HALO + tm, :], wv_ref[...], preferred_element_type=F32)
    a_prev = pltpu.roll(a, 1, 0)[HALO:HALO + tm]
    a_next = pltpu.roll(a, rows - 1, 0)[HALO:HALO + tm]
    cw = cw_ref[...]
    a = cw[0:1] * a_prev + cw[1:2] * a[HALO:HALO + tm] + cw[2:3] * a_next + cb_ref[...]
    gate = 0.5 * a * (1.0 + lax.erf(a * np.float32(np.sqrt(0.5))))
    o_ref[...] += jnp.dot((gate * val).astype(BF16), wo_ref[...], preferred_element_type=F32)


FFN_TM = 512
FFN_TF = 512


def _ffn(x, g, w_in, conv_w, conv_b, w_out, *, plen, slen, tm=FFN_TM, tf=FFN_TF, name):
    m, k = x.shape
    nf = D_FF // tf
    r8 = tm // 8
    nb8 = m // 8
    return pl.pallas_call(
        functools.partial(_ffn_kernel, tm=tm, plen=plen, slen=slen),
        grid=(m // tm, nf),
        in_specs=[
            pl.BlockSpec((tm, k), lambda i, j: (i, 0)),
            pl.BlockSpec((8, k), lambda i, j: (jnp.maximum(i * r8 - 1, 0), 0)),
            pl.BlockSpec((8, k), lambda i, j: (jnp.minimum((i + 1) * r8, nb8 - 1), 0)),
            pl.BlockSpec((1, k), lambda i, j: (0, 0)),
            pl.BlockSpec((k, tf), lambda i, j: (0, j)),
            pl.BlockSpec((k, tf), lambda i, j: (0, nf + j)),
            pl.BlockSpec((3, tf), lambda i, j: (0, j)),
            pl.BlockSpec((1, tf), lambda i, j: (0, j)),
            pl.BlockSpec((tf, k), lambda i, j: (j, 0)),
        ],
        out_specs=pl.BlockSpec((tm, k), lambda i, j: (i, 0)),
        out_shape=jax.ShapeDtypeStruct((m, k), F32),
        scratch_shapes=[pltpu.VMEM((tm + 2 * HALO, k), BF16)],
        compiler_params=_params("parallel", "arbitrary"),
        name=name,
    )(x, x, x, g.reshape(1, k), w_in, w_in, conv_w, conv_b.reshape(1, -1), w_out)


def _rmsnorm_kernel(x_ref, g_ref, o_ref):
    o_ref[...] = _rms_rows(x_ref[...], g_ref[...])


def _rmsnorm(x, g, *, row0, rows, tm=256, name):
    k = x.shape[1]
    return pl.pallas_call(
        _rmsnorm_kernel,
        grid=(rows // tm,),
        in_specs=[pl.BlockSpec((tm, k), lambda i: (row0 // tm + i, 0)),
                  pl.BlockSpec((1, k), lambda i: (0, 0))],
        out_specs=pl.BlockSpec((tm, k), lambda i: (i, 0)),
        out_shape=jax.ShapeDtypeStruct((rows, k), F32),
        compiler_params=_params("parallel"),
        name=name,
    )(x, g.reshape(1, k))


def _alibi_slopes(n):
    return jnp.power(2.0, -8.0 * jnp.arange(1, n + 1, dtype=F32) / n)


def _mixer_a(x, norm, w_qkv, sink, w_o, *, plen, slen, tag):
    nq, nk = A_HEADS * HEAD_DIM, A_KV_HEADS * HEAD_DIM
    colscale = jnp.concatenate([jnp.full((nq,), HEAD_DIM ** -0.5 * LOG2E, F32),
                                jnp.ones((2 * nk,), F32)])
    qkv = _norm_matmul(x, norm, w_qkv.astype(BF16), colscale, name=tag + "_qkv")
    o = _band_attn(qkv, 0, qkv, A_HEADS, qkv, A_HEADS + A_KV_HEADS, _alibi_slopes(A_HEADS) * LOG2E,
                   sink * LOG2E, hq=A_HEADS, hkv=A_KV_HEADS, half=A_WINDOW, dil=1, seg=slen,
                   nseg_p=plen // slen, want_lse=False, name=tag + "_attn")
    return _matmul_res(o, w_o.astype(BF16), x, name=tag + "_wo")


def _lse_to_natural(lse, d, seg):
    h, m = lse.shape
    t = lse.reshape(h, m // seg, d, seg // ROW_TILE, ROW_TILE // d)
    return t.transpose(1, 3, 4, 2, 0).reshape(m, h)


def _mixer_b(x, norm, w_qkv, w_o, *, plen, slen, tag):
    hw = B_HEADS * HEAD_DIM
    colscale = jnp.tile(jnp.concatenate([jnp.full((hw,), HEAD_DIM ** -0.5 * LOG2E, F32),
                                         jnp.ones((2 * hw,), F32)]), len(B_GROUPS))
    nseg_p = plen // slen
    dils = tuple(dil for _, dil in B_GROUPS)
    qkvs = _norm_matmul(x, norm, w_qkv.astype(BF16), colscale, dils=dils, seg=slen,
                        name=tag + "_qkv")
    outs, lses = [], []
    for qkv, (window, dil) in zip(qkvs, B_GROUPS):
        half = (window // 2) // dil
        o, lse = _band_attn(qkv, 0, qkv, B_HEADS, qkv, 2 * B_HEADS,
                            _alibi_slopes(B_HEADS) * (dil * LOG2E), None,
                            hq=B_HEADS, hkv=B_HEADS, half=half, dil=dil, seg=slen, nseg_p=nseg_p,
                            want_lse=True, name="%s_attn_d%d" % (tag, dil))
        outs.append(o)
        lses.append(_lse_to_natural(lse, dil, slen))
    return _merge_res(outs, lses, dils, w_o.astype(BF16), x, seg=slen, name=tag + "_wo")


def _rope_tabs(plen, slen, nsample):
    pos = jnp.arange(plen, dtype=F32)
    inv = jnp.power(ROPE_THETA, -jnp.arange(0, C_ROPE, 2, dtype=F32) / C_ROPE)
    ang = pos[:, None] * inv[None, :]
    cos, sin = jnp.cos(ang), jnp.sin(ang)
    flat = lambda t: jnp.concatenate([t] + [t[:slen]] * nsample, axis=0)
    cos, sin = flat(cos), flat(sin)
    z32, z64 = jnp.zeros_like(cos), jnp.zeros((cos.shape[0], 64), F32)
    return (jnp.concatenate([cos, cos, z64], axis=1),
            jnp.concatenate([-sin, z32, z64], axis=1),
            jnp.concatenate([z32, sin, z64], axis=1))


def _mixer_c(x, norm, w_down, q_norm, kv_norm, w_uq, w_ukv, w_o, *, plen, slen, tag):
    m = x.shape[0]
    nsample = (m - plen) // slen
    tabs = _rope_tabs(plen, slen, nsample)
    w_down_p = jnp.pad(w_down, ((0, 0), (0, 128 - C_ROPE))).astype(BF16)
    cq, ckv, kr = _c_down(x, norm, w_down_p, q_norm, kv_norm, tabs)
    wq = w_uq.reshape(C_Q_RANK, C_HEADS, C_NOPE + C_ROPE)
    wq = jnp.pad(wq, ((0, 0), (0, 0), (0, 256 - C_NOPE - C_ROPE))).reshape(C_Q_RANK, C_HEADS * 256)
    q = _c_q(cq, wq.astype(BF16), tabs, scale=(C_NOPE + C_ROPE) ** -0.5 * LOG2E)
    k, vt = _c_kv(ckv, w_ukv.astype(BF16), kr, tm=FLASH_TK)
    o = _flash(q, k, vt, None, row0=0, seq=plen, nseq=1, name=tag + "_flash_p")
    o = _flash(q, k, vt, o, row0=plen, seq=slen, nseq=nsample, name=tag + "_flash_s")
    return _heads_res(o, w_o.astype(BF16), x, name=tag + "_wo")


def kernel(x_prompt, x_sample, l0_mix_norm, l0_a_w_qkv, l0_a_sink, l0_a_w_o, l0_ffn_norm, l0_ffn_w_in, l0_ffn_conv_w, l0_ffn_conv_b, l0_ffn_w_out, l1_mix_norm, l1_b_w_qkv, l1_b_w_o, l1_ffn_norm, l1_ffn_w_in, l1_ffn_conv_w, l1_ffn_conv_b, l1_ffn_w_out, l2_mix_norm, l2_c_w_down, l2_c_q_norm, l2_c_kv_norm, l2_c_w_uq, l2_c_w_ukv, l2_c_w_o, l2_ffn_norm, l2_ffn_w_in, l2_ffn_conv_w, l2_ffn_conv_b, l2_ffn_w_out, l3_mix_norm, l3_a_w_qkv, l3_a_sink, l3_a_w_o, l3_ffn_norm, l3_ffn_w_in, l3_ffn_conv_w, l3_ffn_conv_b, l3_ffn_w_out, final_norm):
    bp, sp, d = x_prompt.shape
    bs, ss, _ = x_sample.shape
    plen = bp * sp
    assert bp == 1, "prompt rows are treated as one sequence"
    x = jnp.concatenate([x_prompt.reshape(plen, d), x_sample.reshape(bs * ss, d)], axis=0)
    kw = dict(plen=plen, slen=ss)

    def ffn(x, norm, w_in, conv_w, conv_b, w_out, tag):
        return _ffn(x, norm, w_in.astype(BF16), conv_w, conv_b, w_out.astype(BF16), name=tag, **kw)

    x = _mixer_a(x, l0_mix_norm, l0_a_w_qkv, l0_a_sink, l0_a_w_o, tag="l0_a", **kw)
    x = ffn(x, l0_ffn_norm, l0_ffn_w_in, l0_ffn_conv_w, l0_ffn_conv_b, l0_ffn_w_out, "l0_ffn")
    x = _mixer_b(x, l1_mix_norm, l1_b_w_qkv, l1_b_w_o, tag="l1_b", **kw)
    x = ffn(x, l1_ffn_norm, l1_ffn_w_in, l1_ffn_conv_w, l1_ffn_conv_b, l1_ffn_w_out, "l1_ffn")
    x = _mixer_c(x, l2_mix_norm, l2_c_w_down, l2_c_q_norm, l2_c_kv_norm, l2_c_w_uq, l2_c_w_ukv,
                 l2_c_w_o, tag="l2_c", **kw)
    x = ffn(x, l2_ffn_norm, l2_ffn_w_in, l2_ffn_conv_w, l2_ffn_conv_b, l2_ffn_w_out, "l2_ffn")
    x = _mixer_a(x, l3_mix_norm, l3_a_w_qkv, l3_a_sink, l3_a_w_o, tag="l3_a", **kw)
    x = ffn(x, l3_ffn_norm, l3_ffn_w_in, l3_ffn_conv_w, l3_ffn_conv_b, l3_ffn_w_out, "l3_ffn")
    y_p = _rmsnorm(x, final_norm, row0=0, rows=plen, name="final_norm_p")
    y_s = _rmsnorm(x, final_norm, row0=plen, rows=bs * ss, name="final_norm_s")
    return (y_p.reshape(bp, sp, d), y_s.reshape(bs, ss, d))
```
